```python
import math, functools
import jax, jax.numpy as jnp
from jax import lax
import numpy as np

D_MODEL = 1024
BATCH = 4
SEQ = 4096
DEPTH = 2

GRID_W = 64
CTX_LEN = 256
N_MOD = 6
NORM_EPS = 1e-6
D_FF = -(-8 * D_MODEL // (3 * 256)) * 256

D_MIX = D_MODEL
GLA_HEADS = 4
GLA_DV = D_MIX // 2 // GLA_HEADS
GLA_DK = GLA_DV // 2
GLA_LR = 16
GLA_TAU = 16.0
GLA_CHUNK = 64
GLA_HK = GLA_HEADS * GLA_DK
GLA_HV = GLA_HEADS * GLA_DV
GMLP_GROUPS = 4
GMLP_WIDTH = D_MIX // 2
GMLP_GC = GMLP_WIDTH // GMLP_GROUPS
GMLP_CHUNK = 128
AB_STATE_SPLITS = (GLA_HK, GLA_HV, GLA_LR, GLA_LR)
AB_OUT_SPLITS = (GLA_HK, GLA_HV, GMLP_WIDTH, GMLP_WIDTH)
AB_IN = sum(AB_STATE_SPLITS + AB_OUT_SPLITS)

SSD_INNER = 2 * D_MODEL
SSD_HEADDIM = 64
SSD_HEADS = SSD_INNER // SSD_HEADDIM
SSD_GROUPS = 4
SSD_HPG = SSD_HEADS // SSD_GROUPS
SSD_STATE = 128
SSD_CHUNK = 128
SSD_CONV = 5
SSD_GS = SSD_GROUPS * SSD_STATE
SSD_CONV_DIM = SSD_INNER + 2 * SSD_GS
SSD_IN = SSD_CONV_DIM + 2 * SSD_HEADS + SSD_INNER

kernel_name = 'hybrid_gla_gmlp_ssd_prefix_dit'


def rms_norm(x, g):
    xf = x.astype(jnp.float32)
    y = xf * lax.rsqrt(jnp.mean(xf * xf, axis=-1, keepdims=True) + NORM_EPS)
    return y.astype(x.dtype) * g


def group_rms_norm(x, g, groups):
    shp = x.shape
    xf = x.astype(jnp.float32).reshape(shp[:-1] + (groups, shp[-1] // groups))
    y = xf * lax.rsqrt(jnp.mean(xf * xf, axis=-1, keepdims=True) + NORM_EPS)
    return y.reshape(shp).astype(x.dtype) * g


def layer_norm(x, g):
    xf = x.astype(jnp.float32)
    mu = jnp.mean(xf, axis=-1, keepdims=True)
    var = jnp.mean(jnp.square(xf - mu), axis=-1, keepdims=True)
    return ((xf - mu) * lax.rsqrt(var + NORM_EPS)).astype(x.dtype) * g


def modulate(h, shift, scale):
    return h * (1 + scale) + shift


def split_cols(a, sizes):
    idx = [int(i) for i in np.cumsum(sizes)[:-1]]
    return jnp.split(a, idx, axis=-1)


def swiglu(h, w_in, w_out):
    g, u = jnp.split(h @ w_in, 2, axis=-1)
    return (jax.nn.silu(g) * u) @ w_out


def to_col_major(x):
    bsz, t, d = x.shape
    rows = t // GRID_W
    return x.reshape(bsz, rows, GRID_W, d).transpose(0, 2, 1, 3).reshape(bsz, t, d)


def to_row_major(x):
    bsz, t, d = x.shape
    rows = t // GRID_W
    return x.reshape(bsz, GRID_W, rows, d).transpose(0, 2, 1, 3).reshape(bsz, t, d)


def dwconv_centred(x, w, b):
    pad = (w.shape[0] - 1) // 2
    y = lax.conv_general_dilated(x, w.astype(x.dtype)[:, None, :], window_strides=(1,),
                                 padding=[(pad, pad)], dimension_numbers=('NWC', 'WIO', 'NWC'),
                                 feature_group_count=x.shape[-1])
    return y + b


def bidir_prefix_scan(scan_f, scan_b, ctx_f, ctx_b, lat_f, lat_b, s0, ctx_out):
    flip = lambda a: None if a is None else jnp.flip(a, axis=1)
    y_cf, s_cf = scan_f(*ctx_f, s0, ctx_out)
    y_xf, _ = scan_f(*lat_f, s_cf, True)
    y_cb, s_cb = scan_b(*[flip(a) for a in ctx_b], s0, ctx_out)
    y_xb, _ = scan_b(*[flip(a) for a in lat_b], s_cb, True)
    y_x = y_xf + flip(y_xb)
    y_c = y_cf + flip(y_cb) if ctx_out else None
    return y_x, y_c


def gla_chunked(k, v, log_a, q, s0, with_output):
    bsz, t, h, dk = k.shape
    dv = v.shape[-1]
    n = t // GLA_CHUNK
    chunks = lambda a: a.astype(jnp.float32).reshape(bsz, n, GLA_CHUNK, h, a.shape[-1])
    k, v, log_a = chunks(k), chunks(v), chunks(log_a)
    b = jnp.cumsum(log_a, axis=2)
    b_last = b[:, :, -1]
    d_state = jnp.einsum('bnlhk,bnlhv->bnhkv', k * jnp.exp(b_last[:, :, None] - b), v)

    def step(s, inp):
        decay, ds = inp
        return decay[..., None] * s + ds, s

    s_final, s_prev = lax.scan(step, s0, (jnp.moveaxis(jnp.exp(b_last), 1, 0), jnp.moveaxis(d_state, 1, 0)))
    if not with_output:
        return None, s_final
    q = chunks(q)
    s_prev = jnp.moveaxis(s_prev, 0, 1)
    q_dec = q * jnp.exp(b)
    o_inter = jnp.einsum('bnlhk,bnhkv->bnlhv', q_dec, s_prev)
    scores = jnp.einsum('bnlhk,bnshk->bnhls', q_dec, k * jnp.exp(-b))
    tri = jnp.tril(jnp.ones((GLA_CHUNK, GLA_CHUNK), dtype=bool))
    scores = jnp.where(tri, scores, 0.0)
    o_intra = jnp.einsum('bnhls,bnshv->bnlhv', scores, v)
    return (o_inter + o_intra).reshape(bsz, t, h, dv), s_final


def ssd_chunked(x, bm, dt, cm, s0, with_output, a_coef):
    bsz, t = x.shape[:2]
    n = t // SSD_CHUNK
    f = lambda z: z.astype(jnp.float32).reshape((bsz, n, SSD_CHUNK) + z.shape[2:])
    x, bm, dt = f(x), f(bm), f(dt)
    acum = jnp.cumsum(dt * a_coef, axis=2)
    a_last = acum[:, :, -1]
    xdt = x * dt[..., None]
    states = jnp.einsum('bclgn,bclgh,bclghp->bcghpn', bm, jnp.exp(a_last[:, :, None] - acum), xdt)

    def step(s, inp):
        decay, st = inp
        return decay[..., None, None] * s + st, s

    s_final, s_prev = lax.scan(step, s0, (jnp.moveaxis(jnp.exp(a_last), 1, 0), jnp.moveaxis(states, 1, 0)))
    if not with_output:
        return None, s_final
    cm = f(cm)
    s_prev = jnp.moveaxis(s_prev, 0, 1)
    y_off = jnp.einsum('bclgn,bcghpn->bclghp', cm, s_prev) * jnp.exp(acum)[..., None]
    at = jnp.moveaxis(acum, 2, -1)
    seg = at[..., :, None] - at[..., None, :]
    tri = jnp.tril(jnp.ones((SSD_CHUNK, SSD_CHUNK), dtype=bool))
    decay = jnp.exp(jnp.where(tri, seg, -jnp.inf))
    cb = jnp.einsum('bclgn,bcsgn->bcgls', cm, bm)
    y_diag = jnp.einsum('bcgls,bcghls,bcsghp->bclghp', cb, decay, xdt)
    return (y_diag + y_off).reshape((bsz, t) + x.shape[3:]), s_final


def ab_project(h, w_in, gate_w, gate_b, full):
    bsz, t, _ = h.shape
    sizes = AB_STATE_SPLITS + AB_OUT_SPLITS if full else AB_STATE_SPLITS
    parts = split_cols(h @ w_in[:, :sum(sizes)], sizes)
    heads = lambda a: a.reshape(bsz, t, GLA_HEADS, -1)
    k, v, lr_f, lr_b = parts[:4]
    la_f = jax.nn.log_sigmoid((lr_f @ gate_w[0] + gate_b[0]).astype(jnp.float32)) / GLA_TAU
    la_b = jax.nn.log_sigmoid((lr_b @ gate_w[1] + gate_b[1]).astype(jnp.float32)) / GLA_TAU
    state = (heads(k), heads(v), heads(la_f), heads(la_b))
    if not full:
        return state, None
    q, r, u, g = parts[4:]
    return state, (heads(q) * GLA_DK ** -0.5, r, u, g)


def gmlp_chunk_mix(u, v, vnorm_g, spatial_w, spatial_b):
    bsz, t, _ = u.shape
    n = t // GMLP_CHUNK
    u = jax.nn.gelu(u)
    v = layer_norm(jax.nn.gelu(v), vnorm_g).reshape(bsz, n, GMLP_CHUNK, GMLP_GROUPS, GMLP_GC)
    s = jnp.einsum('gts,bnsgc->bntgc', spatial_w, v) + spatial_b.T[:, :, None]
    return u * s.reshape(bsz, t, GMLP_WIDTH)


def mixer_gla_gmlp(hx, hc, w_in, gate_w, gate_b, gla_norm_g, vnorm_g, spatial_w, spatial_b, w_out, ctx_out):
    (kx, vx, lfx, lbx), (qx, rx, ux, gx) = ab_project(hx, w_in, gate_w, gate_b, True)
    (kc, vc, lfc, lbc), rest_c = ab_project(hc, w_in, gate_w, gate_b, ctx_out)
    qc = rest_c[0] if ctx_out else None
    s0 = jnp.zeros((hx.shape[0], GLA_HEADS, GLA_DK, GLA_DV), jnp.float32)
    ox, oc = bidir_prefix_scan(gla_chunked, gla_chunked,
                               [kc, vc, lfc, qc], [kc, vc, lbc, qc],
                               [kx, vx, lfx, qx], [kx, vx, lbx, qx], s0, ctx_out)

    def merge(o, r, u, g):
        bsz, t = o.shape[:2]
        a = group_rms_norm(o.reshape(bsz, t, GLA_HV), gla_norm_g, GLA_HEADS) * jax.nn.silu(r)
        b = gmlp_chunk_mix(u, g, vnorm_g, spatial_w, spatial_b)
        return jnp.concatenate([a.astype(b.dtype), b], axis=-1) @ w_out

    yx = merge(ox, rx, ux, gx)
    yc = merge(oc, rest_c[1], rest_c[2], rest_c[3]) if ctx_out else None
    return yx, yc


def ssd_project(h, w_in, conv_w, conv_b, dt_bias, full):
    bsz, t, _ = h.shape
    if full:
        p = h @ w_in
        xbc = jax.nn.silu(dwconv_centred(p[..., :SSD_CONV_DIM], conv_w, conv_b))
        xs, bm, cm = split_cols(xbc, (SSD_INNER, SSD_GS, SSD_GS))
        dt_raw = p[..., SSD_CONV_DIM:SSD_CONV_DIM + 2 * SSD_HEADS]
        z = p[..., SSD_CONV_DIM + 2 * SSD_HEADS:]
        cm = cm.reshape(bsz, t, SSD_GROUPS, SSD_STATE)
    else:
        nxb = SSD_INNER + SSD_GS
        xb = jax.nn.silu(dwconv_centred(h @ w_in[:, :nxb], conv_w[:, :nxb], conv_b[:nxb]))
        xs, bm = split_cols(xb, (SSD_INNER, SSD_GS))
        dt_raw = h @ w_in[:, SSD_CONV_DIM:SSD_CONV_DIM + 2 * SSD_HEADS]
        cm = None
        z = None
    dt = jax.nn.softplus(dt_raw.astype(jnp.float32).reshape(bsz, t, 2, SSD_HEADS) + dt_bias)
    dt = dt.reshape(bsz, t, 2, SSD_GROUPS, SSD_HPG)
    xs = xs.reshape(bsz, t, SSD_GROUPS, SSD_HPG, SSD_HEADDIM)
    bm = bm.reshape(bsz, t, SSD_GROUPS, SSD_STATE)
    return xs, bm, cm, dt[:, :, 0], dt[:, :, 1], z


def mixer_ssd(hx, hc, w_in, conv_w, conv_b, dt_bias, a_log, d_skip, norm_g, w_out, ctx_out):
    xx, bx, cx, dfx, dbx, zx = ssd_project(hx, w_in, conv_w, conv_b, dt_bias, True)
    xc, bc, cc, dfc, dbc, zc = ssd_project(hc, w_in, conv_w, conv_b, dt_bias, ctx_out)
    a = -jnp.exp(a_log.astype(jnp.float32)).reshape(2, SSD_GROUPS, SSD_HPG)
    s0 = jnp.zeros((hx.shape[0], SSD_GROUPS, SSD_HPG, SSD_HEADDIM, SSD_STATE), jnp.float32)
    yx, yc = bidir_prefix_scan(functools.partial(ssd_chunked, a_coef=a[0]),
                               functools.partial(ssd_chunked, a_coef=a[1]),
                               [xc, bc, dfc, cc], [xc, bc, dbc, cc],
                               [xx, bx, dfx, cx], [xx, bx, dbx, cx], s0, ctx_out)
    d_h = d_skip.reshape(SSD_GROUPS, SSD_HPG)[..., None]

    def finish(y, xs, z):
        bsz, t = y.shape[:2]
        y = (y + d_h * xs).reshape(bsz, t, SSD_INNER)
        y = group_rms_norm(y * jax.nn.silu(z), norm_g, SSD_GROUPS)
        return y.astype(z.dtype) @ w_out

    out_x = finish(yx, xx, zx)
    out_c = finish(yc, xc, zc) if ctx_out else None
    return out_x, out_c


def setup_inputs(seed: int = 0) -> dict:
    key = jax.random.key(seed)
    ks = iter(jax.random.split(key, 40))

    def nrm(shape, scale):
        return jax.random.normal(next(ks), shape, jnp.float32) * scale

    ne, no = (DEPTH + 1) // 2, DEPTH // 2
    dt0 = jnp.exp(jax.random.uniform(next(ks), (no, 2, SSD_HEADS), jnp.float32, math.log(1e-3), math.log(1e-1)))
    return {
        'x': nrm((BATCH, SEQ, D_MODEL), 1.0),
        'c': nrm((BATCH, D_MODEL), 1.0),
        'ctx': nrm((BATCH, CTX_LEN, D_MODEL), 1.0),
        'c_ctx': nrm((D_MODEL,), 1.0),
        'mod_w': nrm((DEPTH, D_MODEL, N_MOD * D_MODEL), D_MODEL ** -0.5),
        'mod_b': nrm((DEPTH, N_MOD * D_MODEL), 0.02),
        'norm_g': 1.0 + nrm((DEPTH, 2, D_MODEL), 0.02),
        'ffn_w_in': nrm((DEPTH, D_MODEL, 2 * D_FF), D_MODEL ** -0.5),
        'ffn_w_out': nrm((DEPTH, D_FF, D_MODEL), D_FF ** -0.5),
        'ab_w_in': nrm((ne, D_MODEL, AB_IN), D_MODEL ** -0.5),
        'ab_gate_w': nrm((ne, 2, GLA_LR, GLA_HK), GLA_LR ** -0.5),
        'ab_gate_b': nrm((ne, 2, GLA_HK), 0.1),
        'ab_gla_norm_g': 1.0 + nrm((ne, GLA_HV), 0.02),
        'ab_vnorm_g': 1.0 + nrm((ne, GMLP_WIDTH), 0.02),
        'ab_spatial_w': nrm((ne, GMLP_GROUPS, GMLP_CHUNK, GMLP_CHUNK), GMLP_CHUNK ** -0.5),
        'ab_spatial_b': 1.0 + nrm((ne, GMLP_GROUPS, GMLP_CHUNK), 0.02),
        'ab_w_out': nrm((ne, D_MIX, D_MODEL), D_MIX ** -0.5),
        'ssd_w_in': nrm((no, D_MODEL, SSD_IN), D_MODEL ** -0.5),
        'ssd_conv_w': nrm((no, SSD_CONV, SSD_CONV_DIM), SSD_CONV ** -0.5),
        'ssd_conv_b': nrm((no, SSD_CONV_DIM), 0.02),
        'ssd_dt_bias': dt0 + jnp.log(-jnp.expm1(-dt0)),
        'ssd_a_log': jnp.log(jax.random.uniform(next(ks), (no, 2, SSD_HEADS), jnp.float32, 1.0, 16.0)),
        'ssd_d': 1.0 + nrm((no, SSD_HEADS), 0.02),
        'ssd_norm_g': 1.0 + nrm((no, SSD_INNER), 0.02),
        'ssd_w_out': nrm((no, SSD_INNER, D_MODEL), SSD_INNER ** -0.5),
        'final_norm_g': 1.0 + nrm((D_MODEL,), 0.02),
    }


def reference(x, c, ctx, c_ctx, mod_w, mod_b, norm_g, ffn_w_in, ffn_w_out,
              ab_w_in, ab_gate_w, ab_gate_b, ab_gla_norm_g, ab_vnorm_g, ab_spatial_w, ab_spatial_b, ab_w_out,
              ssd_w_in, ssd_conv_w, ssd_conv_b, ssd_dt_bias, ssd_a_log, ssd_d, ssd_norm_g, ssd_w_out,
              final_norm_g):
    sc = jax.nn.silu(c)
    sc_ctx = jax.nn.silu(c_ctx)
    for i in range(DEPTH):
        ctx_out = i < DEPTH - 1
        j = i // 2
        mx = jnp.split((sc @ mod_w[i] + mod_b[i])[:, None, :], N_MOD, axis=-1)
        mc = jnp.split(sc_ctx @ mod_w[i] + mod_b[i], N_MOD, axis=-1)
        hx = modulate(rms_norm(x, norm_g[i, 0]), mx[0], mx[1])
        hc = modulate(rms_norm(ctx, norm_g[i, 0]), mc[0], mc[1])
        if i % 2 == 0:
            yx, yc = mixer_gla_gmlp(hx, hc, ab_w_in[j], ab_gate_w[j], ab_gate_b[j], ab_gla_norm_g[j],
                                    ab_vnorm_g[j], ab_spatial_w[j], ab_spatial_b[j], ab_w_out[j], ctx_out)
        else:
            yx_cm, yc = mixer_ssd(to_col_major(hx), hc, ssd_w_in[j], ssd_conv_w[j], ssd_conv_b[j],
                                  ssd_dt_bias[j], ssd_a_log[j], ssd_d[j], ssd_norm_g[j], ssd_w_out[j], ctx_out)
            yx = to_row_major(yx_cm)
        x = x + mx[2] * yx
        x = x + mx[5] * swiglu(modulate(rms_norm(x, norm_g[i, 1]), mx[3], mx[4]), ffn_w_in[i], ffn_w_out[i])
        if ctx_out:
            ctx = ctx + mc[2] * yc
            ctx = ctx + mc[5] * swiglu(modulate(rms_norm(ctx, norm_g[i, 1]), mc[3], mc[4]), ffn_w_in[i], ffn_w_out[i])
    return rms_norm(x, final_norm_g)
```

```python
import functools

import jax
import jax.numpy as jnp
from jax import lax
from jax.experimental import pallas as pl
from jax.experimental.pallas import tpu as pltpu

F32 = jnp.float32
BF16 = jnp.bfloat16

D = 1024
N_MOD = 6
EPS = 1e-6
D_FF = 2816
GRID_W = 64

GLA_H, GLA_DK, GLA_DV, GLA_LR, GLA_CHUNK = 4, 64, 128, 16, 64
GLA_TAU_INV = 1.0 / 16.0
GMLP_W, GMLP_G, GMLP_CHUNK = 512, 4, 128
P0_COLS = 2560
P0_W = P0_COLS + 128

SSD_INNER, SSD_P, SSD_H, SSD_G, SSD_HPG, SSD_N, SSD_CHUNK, SSD_CONV = 2048, 64, 32, 4, 8, 128, 128, 5
SSD_GS = SSD_G * SSD_N
SSD_XBC = SSD_INNER + 2 * SSD_GS
HALO = 8

VMEM_LIMIT = 56 * 1024 * 1024


def _cparams(n_axes):
    return pltpu.CompilerParams(dimension_semantics=("arbitrary",) * n_axes,
                                vmem_limit_bytes=VMEM_LIMIT)


def _dot(a, b):
    return jnp.dot(a, b, preferred_element_type=F32)


def _dot_nt(a, b):
    return lax.dot_general(a, b, (((1,), (1,)), ((), ())), preferred_element_type=F32)


def _dot_tn(a, b):
    return lax.dot_general(a, b, (((0,), (0,)), ((), ())), preferred_element_type=F32)


def _split3(x):
    hi = x.astype(BF16)
    r1 = x - hi.astype(F32)
    mid = r1.astype(BF16)
    lo = (r1 - mid.astype(F32)).astype(BF16)
    return hi, mid, lo


def _dot01_l(t01, x):
    hi, mid, lo = _split3(x)
    return (_dot(t01, lo) + _dot(t01, mid)) + _dot(t01, hi)


def _dot01_r(x, e01):
    hi, mid, lo = _split3(x)
    return (_dot(lo, e01) + _dot(mid, e01)) + _dot(hi, e01)


def _dot01_tn(x, e01):
    hi, mid, lo = _split3(x)
    return (_dot_tn(lo, e01) + _dot_tn(mid, e01)) + _dot_tn(hi, e01)


def _silu(x):
    return x * jax.nn.sigmoid(x)


def _softplus(x):
    return jnp.maximum(x, 0.0) + jnp.log1p(jnp.exp(-jnp.abs(x)))


def _log_sigmoid(x):
    return jnp.minimum(x, 0.0) - jnp.log1p(jnp.exp(-jnp.abs(x)))


def _rms(x, g):
    return x * lax.rsqrt(jnp.mean(x * x, axis=-1, keepdims=True) + EPS) * g


def _norm_mod(x, g, shift, scale):
    return _rms(x, g) * (1.0 + scale) + shift


def _mods_kernel(cc_ref, w_ref, b_ref, o_ref):
    s = _silu(cc_ref[...])
    o_ref[...] = jnp.dot(s, w_ref[...], precision=lax.Precision.HIGHEST,
                         preferred_element_type=F32) + b_ref[...]


def _mods(cc, mod_w, mod_b):
    depth = mod_w.shape[0]
    return pl.pallas_call(
        _mods_kernel,
        grid=(depth, N_MOD),
        in_specs=[pl.BlockSpec((8, D), lambda i, j: (0, 0)),
                  pl.BlockSpec((None, D, D), lambda i, j: (i, 0, j)),
                  pl.BlockSpec((None, 1, D), lambda i, j: (i, 0, j))],
        out_specs=pl.BlockSpec((None, 8, D), lambda i, j: (i, 0, j)),
        out_shape=jax.ShapeDtypeStruct((depth, 8, N_MOD * D), F32),
        compiler_params=_cparams(2),
        name="mods",
    )(cc, mod_w, mod_b.reshape(depth, 1, N_MOD * D))


def _mod_spec(mod_row):
    if mod_row is None:
        return pl.BlockSpec((None, 1, N_MOD * D), lambda b, i: (b, 0, 0))
    return pl.BlockSpec((None, 1, N_MOD * D), lambda b, i: (mod_row, 0, 0))


def _const_spec(shape):
    nd = len(shape)
    return pl.BlockSpec(shape, lambda *_: (0,) * nd, pipeline_mode=pl.Buffered(1))


def _proj0_kernel(x_ref, mod_ref, g_ref, w_ref, gw_ref, gb_ref, p_ref, la_ref):
    h = _norm_mod(x_ref[...], g_ref[...], mod_ref[:, 0:D], mod_ref[:, D:2 * D]).astype(BF16)
    for c0 in range(0, 2048, 512):
        p_ref[:, c0:c0 + 512] = _dot(h, w_ref[:, c0:c0 + 512]).astype(BF16)
    p_ref[:, 2048:2304] = _dot(h, w_ref[:, 2048:2304]).astype(BF16)
    p_ref[:, 2304:2560] = (_dot(h, w_ref[:, 2304:2560]) * (GLA_DK ** -0.5)).astype(BF16)
    lr = _dot(h, w_ref[:, P0_COLS:P0_W])
    z = jnp.dot(lr, gw_ref[...], precision=lax.Precision.HIGHEST,
                preferred_element_type=F32) + gb_ref[...]
    la_ref[...] = _log_sigmoid(z) * GLA_TAU_INV


def _proj0(x, mods, mod_row, g, w, gw, gb, tm):
    bsz, t, _ = x.shape
    return pl.pallas_call(
        _proj0_kernel,
        grid=(bsz, t // tm),
        in_specs=[pl.BlockSpec((None, tm, D), lambda b, i: (b, i, 0)),
                  _mod_spec(mod_row),
                  _const_spec((1, D)),
                  _const_spec((D, P0_W)),
                  _const_spec((128, 512)),
                  _const_spec((1, 512))],
        out_specs=[pl.BlockSpec((None, tm, P0_COLS), lambda b, i: (b, i, 0)),
                   pl.BlockSpec((None, tm, 512), lambda b, i: (b, i, 0))],
        out_shape=[jax.ShapeDtypeStruct((bsz, t, P0_COLS), BF16),
                   jax.ShapeDtypeStruct((bsz, t, 512), F32)],
        compiler_params=_cparams(2),
        name="proj0",
    )(x, mods, g, w, gw, gb)


def _gla_kernel(k_ref, q_ref, v_ref, r_ref, laf_ref, lab_ref, s0f_ref, s0b_ref, gng_ref,
                a_ref, sf_ref, sb_ref, o_acc, *, t):
    L = GLA_CHUNK
    n = t // L
    row = lax.broadcasted_iota(jnp.int32, (L, L), 0)
    col = lax.broadcasted_iota(jnp.int32, (L, L), 1)
    tril, triu = col <= row, col >= row
    tril_b, triu_b = tril.astype(BF16), triu.astype(BF16)
    ones = jnp.ones((L, 128), BF16)
    bd_mask = (lax.broadcasted_iota(jnp.int32, (128, 256), 0) // GLA_DK
               == lax.broadcasted_iota(jnp.int32, (128, 256), 1) // GLA_DV)
    lane = lax.broadcasted_iota(jnp.int32, (L, 128), 1)
    head_masks = (lane < GLA_DK, lane >= GLA_DK)

    sf_ref[...] = s0f_ref[...]
    sb_ref[...] = s0b_ref[...]
    o_acc[...] = jnp.zeros_like(o_acc)

    def one(rows, la_ref, tri, tri_b, last, s_ref):
        k = k_ref[rows, :].astype(F32)
        q = q_ref[rows, :].astype(F32)
        v = v_ref[rows, :]
        la = la_ref[rows, :]
        b = _dot01_l(tri_b, la)
        b_tot = _dot01_tn(la, ones)
        b_last = b[last:last + 1, :]
        qd = q * jnp.exp(b)
        ki = (k * jnp.exp(-b)).astype(BF16)
        kd = (k * jnp.exp(b_last - b)).astype(BF16)
        s_prev = s_ref[...]
        o = _dot(qd.astype(BF16), s_prev.astype(BF16))
        ds = _dot_tn(kd, v)
        dc = jnp.exp(b_tot)
        s_ref[...] = s_prev * jnp.concatenate([dc, dc], axis=1) + jnp.where(bd_mask, ds, 0.0)
        intra = []
        for h in range(2):
            qh = jnp.where(head_masks[h], qd, 0.0).astype(BF16)
            sc = jnp.where(tri, _dot_nt(qh, ki), 0.0)
            intra.append(_dot(sc.astype(BF16), v[:, h * GLA_DV:(h + 1) * GLA_DV]))
        o_acc[rows, :] += o + jnp.concatenate(intra, axis=1)

    def body(i, carry):
        one(pl.ds(pl.multiple_of(i * L, L), L), laf_ref, tril, tril_b, L - 1, sf_ref)
        one(pl.ds(pl.multiple_of((n - 1 - i) * L, L), L), lab_ref, triu, triu_b, 0, sb_ref)
        return carry

    lax.fori_loop(0, n, body, 0)

    fb = min(t, 256)

    def fin(j, carry):
        rows = pl.ds(pl.multiple_of(j * fb, fb), fb)
        o = o_acc[rows, :]
        r = r_ref[rows, :].astype(F32)
        outs = []
        for h in range(2):
            sl = slice(h * GLA_DV, (h + 1) * GLA_DV)
            outs.append(_rms(o[:, sl], gng_ref[:, sl]) * _silu(r[:, sl]))
        a_ref[rows, :] = jnp.concatenate(outs, axis=1).astype(BF16)
        return carry

    lax.fori_loop(0, t // fb, fin, 0)


def _gla(p, la, s0f, s0b, gng):
    bsz, t, _ = p.shape
    npair = GLA_H // 2
    st_spec = pl.BlockSpec((None, None, 128, 256), lambda b, h: (b, h, 0, 0))
    st_shape = jax.ShapeDtypeStruct((bsz, npair, 128, 256), F32)
    return pl.pallas_call(
        functools.partial(_gla_kernel, t=t),
        grid=(bsz, npair),
        in_specs=[pl.BlockSpec((None, t, 128), lambda b, h: (b, 0, 16 + h)),
                  pl.BlockSpec((None, t, 128), lambda b, h: (b, 0, 18 + h)),
                  pl.BlockSpec((None, t, 256), lambda b, h: (b, 0, h)),
                  pl.BlockSpec((None, t, 256), lambda b, h: (b, 0, 2 + h)),
                  pl.BlockSpec((None, t, 128), lambda b, h: (b, 0, h)),
                  pl.BlockSpec((None, t, 128), lambda b, h: (b, 0, 2 + h)),
                  st_spec, st_spec,
                  pl.BlockSpec((1, 256), lambda b, h: (0, h))],
        out_specs=[pl.BlockSpec((None, t, 256), lambda b, h: (b, 0, h)), st_spec, st_spec],
        out_shape=[jax.ShapeDtypeStruct((bsz, t, 512), BF16), st_shape, st_shape],
        scratch_shapes=[pltpu.VMEM((t, 256), F32)],
        compiler_params=_cparams(2),
        name="gla",
    )(p, p, p, p, la, la, s0f, s0b, gng)


def _ffn_tail(x1, mod_ref, n2g_ref, wi_ref, wo2_ref, act_ref):
    h2 = _norm_mod(x1, n2g_ref[...], mod_ref[:, 3 * D:4 * D], mod_ref[:, 4 * D:5 * D]).astype(BF16)
    for c0 in range(0, D_FF, 512):
        c1 = min(c0 + 512, D_FF)
        gate = _dot(h2, wi_ref[:, c0:c1])
        up = _dot(h2, wi_ref[:, D_FF + c0:D_FF + c1])
        act_ref[:, c0:c1] = (_silu(gate) * up).astype(BF16)
    return x1 + mod_ref[:, 5 * D:6 * D] * _dot(act_ref[...], wo2_ref[...])


def _load_tile(x_ref, r, k):
    if k == 1:
        return x_ref[...]
    return jnp.concatenate([x_ref[:, j * D:(j + 1) * D] for j in range(k)], axis=0)


def _store_tile(o_ref, x, r, k):
    if k == 1:
        o_ref[...] = x
    else:
        for j in range(k):
            o_ref[:, j * D:(j + 1) * D] = x[j * r:(j + 1) * r, :]


def _post0_kernel(x_ref, mod_ref, a_ref, u_ref, gv_ref, vng_ref, sw_ref, sbt_ref, wo_ref,
                  n2g_ref, wi_ref, wo2_ref, o_ref, mix_ref, act_ref, *, r, k):
    tm = r * k
    x = _load_tile(x_ref, r, k)
    uu = jax.nn.gelu(u_ref[...].astype(F32), approximate=True)
    vv = jax.nn.gelu(gv_ref[...].astype(F32), approximate=True)
    mu = jnp.mean(vv, axis=-1, keepdims=True)
    vc = vv - mu
    vn = (vc * lax.rsqrt(jnp.mean(vc * vc, axis=-1, keepdims=True) + EPS) * vng_ref[...]).astype(BF16)
    mix_ref[:, 0:512] = a_ref[...]
    for ci in range(tm // GMLP_CHUNK):
        rs = slice(ci * GMLP_CHUNK, (ci + 1) * GMLP_CHUNK)
        for gi in range(GMLP_G):
            cs = slice(gi * 128, (gi + 1) * 128)
            s = _dot(sw_ref[gi], vn[rs, cs]) + sbt_ref[:, gi:gi + 1]
            mix_ref[rs, 512 + gi * 128:512 + (gi + 1) * 128] = (uu[rs, cs] * s).astype(BF16)
    x1 = x + mod_ref[:, 2 * D:3 * D] * _dot(mix_ref[...], wo_ref[...])
    x2 = _ffn_tail(x1, mod_ref, n2g_ref, wi_ref, wo2_ref, act_ref)
    _store_tile(o_ref, x2, r, k)


def _post1_kernel(x_ref, mod_ref, yn_ref, wo_ref, n2g_ref, wi_ref, wo2_ref, fg_ref,
                  o_ref, act_ref, *, r, k):
    x = _load_tile(x_ref, r, k)
    x1 = x + mod_ref[:, 2 * D:3 * D] * _dot(yn_ref[...], wo_ref[...])
    x2 = _ffn_tail(x1, mod_ref, n2g_ref, wi_ref, wo2_ref, act_ref)
    _store_tile(o_ref, _rms(x2, fg_ref[...]), r, k)


def _post0(x, mods, mod_row, a, p, vng, sw, sbt, wo, n2g, wi, wo2, tm):
    bsz, t, _ = x.shape
    tok = lambda w, j: pl.BlockSpec((None, tm, w), lambda b, i: (b, i, j))
    return pl.pallas_call(
        functools.partial(_post0_kernel, r=tm, k=1),
        grid=(bsz, t // tm),
        in_specs=[tok(D, 0), _mod_spec(mod_row), tok(512, 0), tok(512, 2), tok(512, 3),
                  _const_spec((1, 512)), _const_spec((GMLP_G, 128, 128)), _const_spec((128, GMLP_G)),
                  _const_spec((D, D)), _const_spec((1, D)),
                  _const_spec((D, 2 * D_FF)), _const_spec((D_FF, D))],
        out_specs=tok(D, 0),
        out_shape=jax.ShapeDtypeStruct((bsz, t, D), F32),
        scratch_shapes=[pltpu.VMEM((tm, D), BF16), pltpu.VMEM((tm, D_FF), BF16)],
        compiler_params=_cparams(2),
        name="post0",
    )(x, mods, a, p, p, vng, sw, sbt, wo, n2g, wi, wo2)


def _post1(xv, mods, yn, wo, n2g, wi, wo2, fg, r, k):
    bsz, _, width = xv.shape
    ncol = width // D
    tm = r * k
    xspec = pl.BlockSpec((None, r, k * D), lambda b, i: (b, 0, i))
    return pl.pallas_call(
        functools.partial(_post1_kernel, r=r, k=k),
        grid=(bsz, ncol // k),
        in_specs=[xspec, _mod_spec(None),
                  pl.BlockSpec((None, tm, SSD_INNER), lambda b, i: (b, i, 0)),
                  _const_spec((SSD_INNER, D)), _const_spec((1, D)),
                  _const_spec((D, 2 * D_FF)), _const_spec((D_FF, D)), _const_spec((1, D))],
        out_specs=xspec,
        out_shape=jax.ShapeDtypeStruct(xv.shape, F32),
        scratch_shapes=[pltpu.VMEM((tm, D_FF), BF16)],
        compiler_params=_cparams(2),
        name="post1",
    )(xv, mods, yn, wo, n2g, wi, wo2, fg)


def _proj1_kernel(xm_ref, xp_ref, xn_ref, mod_ref, g_ref, wxbc_ref, wz_ref, wdt_ref,
                  cw_ref, cb_ref, dtb_ref, xbc_ref, z_ref, dt_ref, h_ext, p_ext, *, r, k):
    tm = r * k
    i = pl.program_id(1)
    last = pl.num_programs(1) - 1
    g = g_ref[...]
    shift, scale = mod_ref[:, 0:D], mod_ref[:, D:2 * D]
    h_ext[0:HALO, :] = jnp.where(i > 0, _norm_mod(xp_ref[...], g, shift, scale), 0.0)
    for j in range(k):
        h_ext[HALO + j * r:HALO + (j + 1) * r, :] = _norm_mod(xm_ref[:, j * D:(j + 1) * D], g, shift, scale)
    h_ext[HALO + tm:2 * HALO + tm, :] = jnp.where(i < last, _norm_mod(xn_ref[...], g, shift, scale), 0.0)
    hb = h_ext[...].astype(BF16)
    pad = (SSD_CONV - 1) // 2
    for c0 in range(0, SSD_XBC, 512):
        cs = slice(c0, c0 + 512)
        p_ext[...] = _dot(hb, wxbc_ref[:, cs])
        acc = cb_ref[:, cs] + cw_ref[0:1, cs] * p_ext[HALO - pad:HALO - pad + tm, :]
        for j in range(1, SSD_CONV):
            acc = acc + cw_ref[j:j + 1, cs] * p_ext[HALO - pad + j:HALO - pad + j + tm, :]
        xbc_ref[:, cs] = _silu(acc).astype(BF16)
    hm = h_ext[HALO:HALO + tm, :].astype(BF16)
    for c0 in range(0, SSD_INNER, 512):
        z_ref[:, c0:c0 + 512] = _dot(hm, wz_ref[:, c0:c0 + 512]).astype(BF16)
    dt_ref[...] = _softplus(_dot(hm, wdt_ref[...]) + dtb_ref[...])


def _proj1(xv, mods, mod_row, g, wxbc, wz, wdt, cw, cb, dtb, r, k):
    bsz, _, width = xv.shape
    ncol = width // D
    tm = r * k
    t = r * ncol
    tok = lambda w: pl.BlockSpec((None, tm, w), lambda b, i: (b, i, 0))
    return pl.pallas_call(
        functools.partial(_proj1_kernel, r=r, k=k),
        grid=(bsz, ncol // k),
        in_specs=[pl.BlockSpec((None, r, k * D), lambda b, i: (b, 0, i)),
                  pl.BlockSpec((None, HALO, D), lambda b, i: (b, r // HALO - 1, jnp.maximum(i * k - 1, 0))),
                  pl.BlockSpec((None, HALO, D), lambda b, i: (b, 0, jnp.minimum((i + 1) * k, ncol - 1))),
                  _mod_spec(mod_row), _const_spec((1, D)),
                  _const_spec((D, SSD_XBC)), _const_spec((D, SSD_INNER)), _const_spec((D, 128)),
                  _const_spec((SSD_CONV, SSD_XBC)), _const_spec((1, SSD_XBC)), _const_spec((1, 128))],
        out_specs=[tok(SSD_XBC), tok(SSD_INNER), tok(128)],
        out_shape=[jax.ShapeDtypeStruct((bsz, t, SSD_XBC), BF16),
                   jax.ShapeDtypeStruct((bsz, t, SSD_INNER), BF16),
                   jax.ShapeDtypeStruct((bsz, t, 128), F32)],
        scratch_shapes=[pltpu.VMEM((tm + 2 * HALO, D), F32), pltpu.VMEM((tm + 2 * HALO, 512), F32)],
        compiler_params=_cparams(2),
        name="proj1",
    )(xv, xv, xv, mods, g, wxbc, wz, wdt, cw, cb, dtb)


def _ssd_kernel(*refs, t, with_output):
    if with_output:
        (xs_ref, b_ref, c_ref, dt_ref, z_ref, alog_ref, dexp_ref, ng_ref, s0f_ref, s0b_ref,
         yn_ref, sf_ref, sb_ref, y_acc) = refs
    else:
        xs_ref, b_ref, dt_ref, alog_ref, s0f_ref, s0b_ref, sf_ref, sb_ref = refs
    L = SSD_CHUNK
    n = t // L
    grp = pl.program_id(1)
    row = lax.broadcasted_iota(jnp.int32, (L, L), 0)
    col = lax.broadcasted_iota(jnp.int32, (L, L), 1)
    tril, triu = col <= row, col >= row
    tril_b, triu_b = tril.astype(BF16), triu.astype(BF16)
    gw = SSD_HPG * SSD_P
    expand = (lax.broadcasted_iota(jnp.int32, (128, gw), 0)
              == lax.broadcasted_iota(jnp.int32, (128, gw), 1) // SSD_P).astype(BF16)
    lane = lax.broadcasted_iota(jnp.int32, (L, 128), 1)
    lo_half = lane < SSD_P
    a_row = -jnp.exp(alog_ref[...])

    sf_ref[...] = s0f_ref[...]
    sb_ref[...] = s0b_ref[...]
    if with_output:
        y_acc[...] = jnp.zeros_like(y_acc)

    def one(rows, d, tri, tri_b, last, s_ref):
        dtc = dt_ref[rows, :]
        acum_all = _dot01_l(tri_b, dtc * a_row)
        shift = (128 - d * SSD_H - grp * SSD_HPG) % 128
        acum = pltpu.roll(acum_all, shift, axis=1)
        dtg = pltpu.roll(dtc, shift, axis=1)
        a_last = acum[last:last + 1, :]
        stack = [dtg, jnp.exp(a_last - acum) * dtg, jnp.exp(acum),
                 jnp.broadcast_to(jnp.exp(a_last), (8, 128))]
        ex = _dot01_r(jnp.concatenate(stack, axis=0), expand)
        dt_e, w_e, ea_e, dec_e = ex[0:L], ex[L:2 * L], ex[2 * L:3 * L], ex[3 * L:3 * L + 1]
        xs = xs_ref[rows, :].astype(F32)
        bc = b_ref[rows, :]
        s_prev = s_ref[...]
        s_ref[...] = s_prev * dec_e + _dot_tn(bc, (xs * w_e).astype(BF16))
        if not with_output:
            return
        cc = c_ref[rows, :]
        xdt = xs * dt_e
        cb = _dot_nt(cc, bc)
        y = _dot(cc, s_prev.astype(BF16)) * ea_e
        acum_t = acum.T
        pieces = []
        for hp in range(SSD_HPG // 2):
            xpair = xdt[:, hp * 128:(hp + 1) * 128]
            acc = None
            for sub in range(2):
                hh = 2 * hp + sub
                seg = acum[:, hh:hh + 1] - acum_t[hh:hh + 1, :]
                m = (cb * jnp.exp(jnp.where(tri, seg, -1e30))).astype(BF16)
                xh = jnp.where(lo_half if sub == 0 else jnp.logical_not(lo_half), xpair, 0.0)
                part = _dot(m, xh.astype(BF16))
                acc = part if acc is None else acc + part
            pieces.append(acc)
        y_acc[rows, :] += y + jnp.concatenate(pieces, axis=1)

    def body(i, carry):
        one(pl.ds(pl.multiple_of(i * L, L), L), 0, tril, tril_b, L - 1, sf_ref)
        one(pl.ds(pl.multiple_of((n - 1 - i) * L, L), L), 1, triu, triu_b, 0, sb_ref)
        return carry

    lax.fori_loop(0, n, body, 0)

    if with_output:
        fb = min(t, 256)

        def fin(j, carry):
            rows = pl.ds(pl.multiple_of(j * fb, fb), fb)
            y = y_acc[rows, :] + dexp_ref[...] * xs_ref[rows, :].astype(F32)
            y = y * _silu(z_ref[rows, :].astype(F32))
            yn_ref[rows, :] = _rms(y, ng_ref[...]).astype(BF16)
            return carry

        lax.fori_loop(0, t // fb, fin, 0)


def _ssd(xbc, dt, z, alog, dexp, ng, s0f, s0b, with_output):
    bsz, t, _ = xbc.shape
    gw = SSD_HPG * SSD_P
    st_spec = pl.BlockSpec((None, None, SSD_N, gw), lambda b, g: (b, g, 0, 0))
    st_shape = jax.ShapeDtypeStruct((bsz, SSD_G, SSD_N, gw), F32)
    xs_spec = pl.BlockSpec((None, t, gw), lambda b, g: (b, 0, g))
    b_spec = pl.BlockSpec((None, t, SSD_N), lambda b, g: (b, 0, SSD_INNER // SSD_N + g))
    c_spec = pl.BlockSpec((None, t, SSD_N), lambda b, g: (b, 0, (SSD_INNER + SSD_GS) // SSD_N + g))
    dt_spec = pl.BlockSpec((None, t, 128), lambda b, g: (b, 0, 0))
    row128 = pl.BlockSpec((1, 128), lambda b, g: (0, 0))
    grow = pl.BlockSpec((1, gw), lambda b, g: (0, g))
    if with_output:
        in_specs = [xs_spec, b_spec, c_spec, dt_spec, xs_spec, row128, grow, grow, st_spec, st_spec]
        args = (xbc, xbc, xbc, dt, z, alog, dexp, ng, s0f, s0b)
        out_specs = [xs_spec, st_spec, st_spec]
        out_shape = [jax.ShapeDtypeStruct((bsz, t, SSD_INNER), BF16), st_shape, st_shape]
        scratch = [pltpu.VMEM((t, gw), F32)]
    else:
        in_specs = [xs_spec, b_spec, dt_spec, row128, st_spec, st_spec]
        args = (xbc, xbc, dt, alog, s0f, s0b)
        out_specs = [st_spec, st_spec]
        out_shape = [st_shape, st_shape]
        scratch = []
    return pl.pallas_call(
        functools.partial(_ssd_kernel, t=t, with_output=with_output),
        grid=(bsz, SSD_G),
        in_specs=in_specs, out_specs=out_specs, out_shape=out_shape, scratch_shapes=scratch,
        compiler_params=_cparams(2),
        name="ssd" if with_output else "ssd_state",
    )(*args)


def _tile(t, pref):
    return pref if t % pref == 0 else t


def kernel(x, c, ctx, c_ctx, mod_w, mod_b, norm_g, ffn_w_in, ffn_w_out, ab_w_in, ab_gate_w, ab_gate_b,
           ab_gla_norm_g, ab_vnorm_g, ab_spatial_w, ab_spatial_b, ab_w_out, ssd_w_in, ssd_conv_w,
           ssd_conv_b, ssd_dt_bias, ssd_a_log, ssd_d, ssd_norm_g, ssd_w_out, final_norm_g):
    bsz, t, _ = x.shape
    tc = ctx.shape[1]
    assert mod_w.shape[0] == 2 and bsz <= 7 and t % (GRID_W * 8) == 0 and tc % 128 == 0
    ctx_row = bsz

    cc = jnp.zeros((8, D), F32).at[:bsz].set(c).at[ctx_row].set(c_ctx)
    mods = _mods(cc, mod_w, mod_b)
    m0 = mods[0].reshape(8, 1, N_MOD * D)
    m1 = mods[1].reshape(8, 1, N_MOD * D)

    w = ab_w_in[0]
    o_k, o_v, o_lf, o_lb, o_q, o_r, o_u, o_g = 0, 256, 768, 784, 800, 1056, 1568, 2080
    w0 = jnp.concatenate([w[:, o_v:o_v + 512], w[:, o_r:o_r + 512], w[:, o_u:o_u + 512],
                          w[:, o_g:o_g + 512], w[:, o_k:o_k + 256], w[:, o_q:o_q + 256],
                          w[:, o_lf:o_lf + 32], jnp.zeros((D, 96), F32)], axis=1).astype(BF16)
    gw = jnp.zeros((128, 512), F32)
    gw = gw.at[0:GLA_LR, 0:256].set(ab_gate_w[0, 0]).at[GLA_LR:2 * GLA_LR, 256:512].set(ab_gate_w[0, 1])
    gb = ab_gate_b[0].reshape(1, 512)
    gng = ab_gla_norm_g[0].reshape(1, 512)
    vng = ab_vnorm_g[0].reshape(1, 512)
    sw = ab_spatial_w[0].astype(BF16)
    sbt = ab_spatial_b[0].T
    wo0 = ab_w_out[0].astype(BF16)
    n1g = [norm_g[i, 0].reshape(1, D) for i in range(2)]
    n2g = [norm_g[i, 1].reshape(1, D) for i in range(2)]
    wi = [ffn_w_in[i].astype(BF16) for i in range(2)]
    wo2 = [ffn_w_out[i].astype(BF16) for i in range(2)]

    zeros_gla = jnp.zeros((bsz, GLA_H // 2, 128, 256), F32)
    tmc = _tile(tc, 256)
    pc, lac = _proj0(ctx, m0, ctx_row, n1g[0], w0, gw, gb, tmc)
    ac, sfc, sbc = _gla(pc, lac, zeros_gla, zeros_gla, gng)
    ctx1 = _post0(ctx, m0, ctx_row, ac, pc, vng, sw, sbt, wo0, n2g[0], wi[0], wo2[0], tmc)

    tmx = _tile(t, 512)
    px, lax_ = _proj0(x, m0, None, n1g[0], w0, gw, gb, tmx)
    ax, _, _ = _gla(px, lax_, sfc, sbc, gng)
    x1 = _post0(x, m0, None, ax, px, vng, sw, sbt, wo0, n2g[0], wi[0], wo2[0], tmx)

    w = ssd_w_in[0]
    wxbc = w[:, :SSD_XBC].astype(BF16)
    wdt = jnp.concatenate([w[:, SSD_XBC:SSD_XBC + 2 * SSD_H], jnp.zeros((D, 64), F32)], axis=1).astype(BF16)
    wz = w[:, SSD_XBC + 2 * SSD_H:].astype(BF16)
    cw = ssd_conv_w[0]
    cb = ssd_conv_b[0].reshape(1, SSD_XBC)
    dtb = jnp.concatenate([ssd_dt_bias[0].reshape(1, 2 * SSD_H), jnp.zeros((1, 64), F32)], axis=1)
    alog = jnp.concatenate([ssd_a_log[0].reshape(1, 2 * SSD_H), jnp.zeros((1, 64), F32)], axis=1)
    dexp = jnp.repeat(ssd_d[0], SSD_P).reshape(1, SSD_INNER)
    sng = ssd_norm_g[0].reshape(1, SSD_INNER)
    wo1 = ssd_w_out[0].astype(BF16)

    zeros_ssd = jnp.zeros((bsz, SSD_G, SSD_N, SSD_HPG * SSD_P), F32)
    xbc_c, _, dt_c = _proj1(ctx1, m1, ctx_row, n1g[1], wxbc, wz, wdt, cw, cb, dtb, tc, 1)
    sfc, sbc = _ssd(xbc_c, dt_c, None, alog, dexp, sng, zeros_ssd, zeros_ssd, False)

    rows = t // GRID_W
    xv = x1.reshape(bsz, rows, GRID_W * D)
    kcol = 512 // rows if 512 % rows == 0 and GRID_W % max(512 // rows, 1) == 0 else 1
    xbc_x, z_x, dt_x = _proj1(xv, m1, None, n1g[1], wxbc, wz, wdt, cw, cb, dtb, rows, kcol)
    yn, _, _ = _ssd(xbc_x, dt_x, z_x, alog, dexp, sng, sfc, sbc, True)
    out = _post1(xv, m1, yn, wo1, n2g[1], wi[1], wo2[1], final_norm_g.reshape(1, D), rows, kcol)
    return out.reshape(bsz, t, D)
```

```python
import functools
import itertools

import jax
import jax.numpy as jnp
from jax import lax
from jax.experimental import pallas as pl
from jax.experimental.pallas import tpu as pltpu

F32 = jnp.float32
BF16 = jnp.bfloat16

D = 1024
N_MOD = 6
EPS = 1e-6
LOG2E = 1.4426950408889634
D_FF = 2816
GRID_W = 64

GLA_H, GLA_DK, GLA_DV, GLA_LR, GLA_CHUNK = 4, 64, 128, 16, 64
GLA_TAU_INV = 1.0 / 16.0
GMLP_W, GMLP_G, GMLP_CHUNK = 512, 4, 128
P0_COLS = 2560
P0_W = P0_COLS + 128

SSD_INNER, SSD_P, SSD_H, SSD_G, SSD_HPG, SSD_N, SSD_CHUNK, SSD_CONV = 2048, 64, 32, 4, 8, 128, 128, 5
SSD_GS = SSD_G * SSD_N
SSD_XBC = SSD_INNER + 2 * SSD_GS
HALO = 8

VMEM_LIMIT = 56 * 1024 * 1024


def _cparams(n_axes):
    return pltpu.CompilerParams(dimension_semantics=("arbitrary",) * n_axes,
                                vmem_limit_bytes=VMEM_LIMIT)


def _dot(a, b):
    return jnp.dot(a, b, preferred_element_type=F32)


def _dot_nt(a, b):
    return lax.dot_general(a, b, (((1,), (1,)), ((), ())), preferred_element_type=F32)


def _dot_tn(a, b):
    return lax.dot_general(a, b, (((0,), (0,)), ((), ())), preferred_element_type=F32)


def _split3(x):
    hi = x.astype(BF16)
    r1 = x - hi.astype(F32)
    mid = r1.astype(BF16)
    lo = (r1 - mid.astype(F32)).astype(BF16)
    return hi, mid, lo


def _dot01_l(t01, x):
    hi, mid, lo = _split3(x)
    return (_dot(t01, lo) + _dot(t01, mid)) + _dot(t01, hi)


def _lockstep(*stages):
    for _ in itertools.zip_longest(*stages):
        pass


def _silu(x):
    return x * jax.nn.sigmoid(x)


def _softplus(x):
    return jnp.maximum(x, 0.0) + jnp.log1p(jnp.exp(-jnp.abs(x)))


def _log_sigmoid(x):
    return jnp.minimum(x, 0.0) - jnp.log1p(jnp.exp(-jnp.abs(x)))


def _rms(x, g):
    return x * lax.rsqrt(jnp.mean(x * x, axis=-1, keepdims=True) + EPS) * g


def _norm_mod(x, g, shift, scale):
    return _rms(x, g) * (1.0 + scale) + shift


def _mods_kernel(cc_ref, w_ref, b_ref, o_ref):
    s = _silu(cc_ref[...])
    o_ref[...] = jnp.dot(s, w_ref[...], precision=lax.Precision.HIGHEST,
                         preferred_element_type=F32) + b_ref[...]


def _mods(cc, mod_w, mod_b):
    depth = mod_w.shape[0]
    return pl.pallas_call(
        _mods_kernel,
        grid=(depth, N_MOD),
        in_specs=[pl.BlockSpec((8, D), lambda i, j: (0, 0)),
                  pl.BlockSpec((None, D, D), lambda i, j: (i, 0, j)),
                  pl.BlockSpec((None, 1, D), lambda i, j: (i, 0, j))],
        out_specs=pl.BlockSpec((None, 8, D), lambda i, j: (i, 0, j)),
        out_shape=jax.ShapeDtypeStruct((depth, 8, N_MOD * D), F32),
        compiler_params=_cparams(2),
        name="mods",
    )(cc, mod_w, mod_b.reshape(depth, 1, N_MOD * D))


def _mod_spec(mod_row):
    if mod_row is None:
        return pl.BlockSpec((None, 1, N_MOD * D), lambda b, i: (b, 0, 0))
    return pl.BlockSpec((None, 1, N_MOD * D), lambda b, i: (mod_row, 0, 0))


def _const_spec(shape):
    nd = len(shape)
    return pl.BlockSpec(shape, lambda *_: (0,) * nd, pipeline_mode=pl.Buffered(1))


def _proj0_kernel(x_ref, mod_ref, g_ref, w_ref, gw_ref, gb_ref, p_ref, la_ref):
    h = _norm_mod(x_ref[...], g_ref[...], mod_ref[:, 0:D], mod_ref[:, D:2 * D]).astype(BF16)
    for c0 in range(0, 2048, 512):
        p_ref[:, c0:c0 + 512] = _dot(h, w_ref[:, c0:c0 + 512]).astype(BF16)
    p_ref[:, 2048:2304] = _dot(h, w_ref[:, 2048:2304]).astype(BF16)
    p_ref[:, 2304:2560] = (_dot(h, w_ref[:, 2304:2560]) * (GLA_DK ** -0.5)).astype(BF16)
    lr = _dot(h, w_ref[:, P0_COLS:P0_W])
    z = _dot(lr.astype(BF16), gw_ref[...]) + gb_ref[...]
    la_ref[...] = _log_sigmoid(z) * GLA_TAU_INV


def _proj0(x, mods, mod_row, g, w, gw, gb, tm):
    bsz, t, _ = x.shape
    return pl.pallas_call(
        _proj0_kernel,
        grid=(bsz, t // tm),
        in_specs=[pl.BlockSpec((None, tm, D), lambda b, i: (b, i, 0)),
                  _mod_spec(mod_row),
                  _const_spec((1, D)),
                  _const_spec((D, P0_W)),
                  _const_spec((128, 512)),
                  _const_spec((1, 512))],
        out_specs=[pl.BlockSpec((None, tm, P0_COLS), lambda b, i: (b, i, 0)),
                   pl.BlockSpec((None, tm, 512), lambda b, i: (b, i, 0))],
        out_shape=[jax.ShapeDtypeStruct((bsz, t, P0_COLS), BF16),
                   jax.ShapeDtypeStruct((bsz, t, 512), F32)],
        compiler_params=_cparams(2),
        name="proj0",
    )(x, mods, g, w, gw, gb)


def _gla_kernel(k_ref, q_ref, v_ref, r_ref, laf_ref, lab_ref, s0f_ref, s0b_ref, gng_ref,
                a_ref, sf_ref, sb_ref, o_acc, *, t):
    L = GLA_CHUNK
    n = t // L
    row = lax.broadcasted_iota(jnp.int32, (L, L), 0)
    col = lax.broadcasted_iota(jnp.int32, (L, L), 1)
    tril, triu = col <= row, col >= row
    tril_b, triu_b = tril.astype(BF16), triu.astype(BF16)
    bd_mask = (lax.broadcasted_iota(jnp.int32, (256, 128), 0) // GLA_DV
               == lax.broadcasted_iota(jnp.int32, (256, 128), 1) // GLA_DK)
    lane = lax.broadcasted_iota(jnp.int32, (L, 128), 1)
    head_masks = (lane < GLA_DK, lane >= GLA_DK)

    sf_ref[...] = s0f_ref[...]
    sb_ref[...] = s0b_ref[...]

    def one(rows, la_ref, tri, tri_b, last, s_ref, finish):
        k = k_ref[rows, :].astype(F32)
        q = q_ref[rows, :].astype(F32)
        v = v_ref[rows, :]
        b = _dot01_l(tri_b, la_ref[rows, :])
        yield
        b_last = b[last:last + 1, :]
        qd = q * jnp.exp(b)
        ki = (k * jnp.exp(-b)).astype(BF16)
        kd = (k * jnp.exp(b_last - b)).astype(BF16)
        sc = [_dot_nt(jnp.where(head_masks[h], qd, 0.0).astype(BF16), ki) for h in range(2)]
        s_prev = s_ref[...]
        o = _dot_nt(qd.astype(BF16), s_prev.astype(BF16))
        s_ref[...] = s_prev * jnp.exp(b_last) + jnp.where(bd_mask, _dot_tn(v, kd), 0.0)
        yield
        intra = [_dot(jnp.where(tri, sc[h], 0.0).astype(BF16), v[:, h * GLA_DV:(h + 1) * GLA_DV])
                 for h in range(2)]
        yield
        o = o + jnp.concatenate(intra, axis=1)
        if not finish:
            o_acc[rows, :] = o
            return
        o = o + o_acc[rows, :]
        r = r_ref[rows, :].astype(F32)
        outs = []
        for h in range(2):
            sl = slice(h * GLA_DV, (h + 1) * GLA_DV)
            outs.append(_rms(o[:, sl], gng_ref[:, sl]) * _silu(r[:, sl]))
        a_ref[rows, :] = jnp.concatenate(outs, axis=1).astype(BF16)

    unroll = 4 if n % 8 == 0 else 2

    def body(finish, iu, carry):
        stages = []
        for u in range(unroll):
            i = iu * unroll + u
            stages.append(one(pl.ds(pl.multiple_of(i * L, L), L), laf_ref, tril, tril_b, L - 1, sf_ref,
                              finish))
            stages.append(one(pl.ds(pl.multiple_of((n - 1 - i) * L, L), L), lab_ref, triu, triu_b, 0,
                              sb_ref, finish))
        _lockstep(*stages)
        return carry

    half = n // (2 * unroll)
    lax.fori_loop(0, half, functools.partial(body, False), 0)
    lax.fori_loop(half, 2 * half, functools.partial(body, True), 0)


def _gla(p, la, s0f, s0b, gng):
    bsz, t, _ = p.shape
    assert (t // GLA_CHUNK) % 4 == 0
    npair = GLA_H // 2
    st_spec = pl.BlockSpec((None, None, 256, 128), lambda b, h: (b, h, 0, 0))
    st_shape = jax.ShapeDtypeStruct((bsz, npair, 256, 128), F32)
    return pl.pallas_call(
        functools.partial(_gla_kernel, t=t),
        grid=(bsz, npair),
        in_specs=[pl.BlockSpec((None, t, 128), lambda b, h: (b, 0, 16 + h)),
                  pl.BlockSpec((None, t, 128), lambda b, h: (b, 0, 18 + h)),
                  pl.BlockSpec((None, t, 256), lambda b, h: (b, 0, h)),
                  pl.BlockSpec((None, t, 256), lambda b, h: (b, 0, 2 + h)),
                  pl.BlockSpec((None, t, 128), lambda b, h: (b, 0, h)),
                  pl.BlockSpec((None, t, 128), lambda b, h: (b, 0, 2 + h)),
                  st_spec, st_spec,
                  pl.BlockSpec((1, 256), lambda b, h: (0, h))],
        out_specs=[pl.BlockSpec((None, t, 256), lambda b, h: (b, 0, h)), st_spec, st_spec],
        out_shape=[jax.ShapeDtypeStruct((bsz, t, 512), BF16), st_shape, st_shape],
        scratch_shapes=[pltpu.VMEM((t, 256), F32)],
        compiler_params=_cparams(2),
        name="gla",
    )(p, p, p, p, la, la, s0f, s0b, gng)


def _ffn_tail(x1, mod_ref, n2g_ref, wi_ref, wo2_ref, act_ref):
    h2 = _norm_mod(x1, n2g_ref[...], mod_ref[:, 3 * D:4 * D], mod_ref[:, 4 * D:5 * D]).astype(BF16)
    for c0 in range(0, D_FF, 512):
        c1 = min(c0 + 512, D_FF)
        gate = _dot(h2, wi_ref[:, c0:c1])
        up = _dot(h2, wi_ref[:, D_FF + c0:D_FF + c1])
        act_ref[:, c0:c1] = (_silu(gate) * up).astype(BF16)
    return x1 + mod_ref[:, 5 * D:6 * D] * _dot(act_ref[...], wo2_ref[...])


def _load_tile(x_ref, r, k):
    if k == 1:
        return x_ref[...]
    return jnp.concatenate([x_ref[:, j, :] for j in range(k)], axis=0)


def _store_tile(o_ref, x, r, k):
    if k == 1:
        o_ref[...] = x
    else:
        for j in range(k):
            o_ref[:, j, :] = x[j * r:(j + 1) * r, :]


def _post0_kernel(x_ref, mod_ref, a_ref, u_ref, gv_ref, vng_ref, sw_ref, sbt_ref, wo_ref,
                  n2g_ref, wi_ref, wo2_ref, o_ref, mix_ref, act_ref, *, r, k):
    tm = r * k
    x = _load_tile(x_ref, r, k)
    uu = jax.nn.gelu(u_ref[...].astype(F32), approximate=True)
    vv = jax.nn.gelu(gv_ref[...].astype(F32), approximate=True)
    mu = jnp.mean(vv, axis=-1, keepdims=True)
    vc = vv - mu
    vn = (vc * lax.rsqrt(jnp.mean(vc * vc, axis=-1, keepdims=True) + EPS) * vng_ref[...]).astype(BF16)
    mix_ref[:, 0:512] = a_ref[...]
    for ci in range(tm // GMLP_CHUNK):
        rs = slice(ci * GMLP_CHUNK, (ci + 1) * GMLP_CHUNK)
        for gi in range(GMLP_G):
            cs = slice(gi * 128, (gi + 1) * 128)
            s = _dot(sw_ref[gi], vn[rs, cs]) + sbt_ref[:, gi:gi + 1]
            mix_ref[rs, 512 + gi * 128:512 + (gi + 1) * 128] = (uu[rs, cs] * s).astype(BF16)
    x1 = x + mod_ref[:, 2 * D:3 * D] * _dot(mix_ref[...], wo_ref[...])
    x2 = _ffn_tail(x1, mod_ref, n2g_ref, wi_ref, wo2_ref, act_ref)
    _store_tile(o_ref, x2, r, k)


def _post1_kernel(x_ref, mod_ref, yn_ref, wo_ref, n2g_ref, wi_ref, wo2_ref, fg_ref,
                  o_ref, act_ref, *, r, k):
    x = _load_tile(x_ref, r, k)
    x1 = x + mod_ref[:, 2 * D:3 * D] * _dot(yn_ref[...], wo_ref[...])
    x2 = _ffn_tail(x1, mod_ref, n2g_ref, wi_ref, wo2_ref, act_ref)
    _store_tile(o_ref, _rms(x2, fg_ref[...]), r, k)


def _post0(x, mods, mod_row, a, p, vng, sw, sbt, wo, n2g, wi, wo2, tm):
    bsz, t, _ = x.shape
    tok = lambda w, j: pl.BlockSpec((None, tm, w), lambda b, i: (b, i, j))
    return pl.pallas_call(
        functools.partial(_post0_kernel, r=tm, k=1),
        grid=(bsz, t // tm),
        in_specs=[tok(D, 0), _mod_spec(mod_row), tok(512, 0), tok(512, 2), tok(512, 3),
                  _const_spec((1, 512)), _const_spec((GMLP_G, 128, 128)), _const_spec((128, GMLP_G)),
                  _const_spec((D, D)), _const_spec((1, D)),
                  _const_spec((D, 2 * D_FF)), _const_spec((D_FF, D))],
        out_specs=tok(D, 0),
        out_shape=jax.ShapeDtypeStruct((bsz, t, D), F32),
        scratch_shapes=[pltpu.VMEM((tm, D), BF16), pltpu.VMEM((tm, D_FF), BF16)],
        compiler_params=_cparams(2),
        name="post0",
    )(x, mods, a, p, p, vng, sw, sbt, wo, n2g, wi, wo2)


def _post1(xv, mods, yn, wo, n2g, wi, wo2, fg, k):
    bsz, r, ncol, _ = xv.shape
    tm = r * k
    xspec = pl.BlockSpec((None, r, k, D), lambda b, i: (b, 0, i, 0))
    return pl.pallas_call(
        functools.partial(_post1_kernel, r=r, k=k),
        grid=(bsz, ncol // k),
        in_specs=[xspec, _mod_spec(None),
                  pl.BlockSpec((None, tm, SSD_INNER), lambda b, i: (b, i, 0)),
                  _const_spec((SSD_INNER, D)), _const_spec((1, D)),
                  _const_spec((D, 2 * D_FF)), _const_spec((D_FF, D)), _const_spec((1, D))],
        out_specs=xspec,
        out_shape=jax.ShapeDtypeStruct(xv.shape, F32),
        scratch_shapes=[pltpu.VMEM((tm, D_FF), BF16)],
        compiler_params=_cparams(2),
        name="post1",
    )(xv, mods, yn, wo, n2g, wi, wo2, fg)


def _proj1_kernel(xm_ref, xp_ref, xn_ref, mod_ref, g_ref, wxbc_ref, wz_ref, wdt_ref,
                  cw_ref, cb_ref, dtb_ref, xbc_ref, z_ref, dt_ref, h_ext, p_ext, *, r, k):
    tm = r * k
    i = pl.program_id(1)
    last = pl.num_programs(1) - 1
    g = g_ref[...]
    shift, scale = mod_ref[:, 0:D], mod_ref[:, D:2 * D]
    xp = xp_ref[...] if k == 1 else xp_ref[:, k - 1, :]
    xn = xn_ref[...] if k == 1 else xn_ref[:, 0, :]
    h_ext[0:HALO, :] = jnp.where(i > 0, _norm_mod(xp, g, shift, scale), 0.0)
    for j in range(k):
        xj = xm_ref[...] if k == 1 else xm_ref[:, j, :]
        h_ext[HALO + j * r:HALO + (j + 1) * r, :] = _norm_mod(xj, g, shift, scale)
    h_ext[HALO + tm:2 * HALO + tm, :] = jnp.where(i < last, _norm_mod(xn, g, shift, scale), 0.0)
    hb = h_ext[...].astype(BF16)
    pad = (SSD_CONV - 1) // 2
    nslab = p_ext.shape[0]
    cwid = nslab * 128
    for c0 in range(0, SSD_XBC, cwid):
        res = _dot(hb, wxbc_ref[:, c0:c0 + cwid])
        for s in range(nslab):
            p_ext[s] = res[:, s * 128:(s + 1) * 128]
        for s in range(nslab):
            cs = slice(c0 + s * 128, c0 + (s + 1) * 128)
            acc = cb_ref[:, cs] + cw_ref[0:1, cs] * p_ext[s, HALO - pad:HALO - pad + tm, :]
            for j in range(1, SSD_CONV):
                acc = acc + cw_ref[j:j + 1, cs] * p_ext[s, HALO - pad + j:HALO - pad + j + tm, :]
            xbc_ref[:, cs] = _silu(acc).astype(BF16)
    hm = h_ext[HALO:HALO + tm, :].astype(BF16)
    for c0 in range(0, SSD_INNER, 512):
        z_ref[:, c0:c0 + 512] = _dot(hm, wz_ref[:, c0:c0 + 512]).astype(BF16)
    dt_ref[...] = _softplus(_dot(hm, wdt_ref[...]) + dtb_ref[...])


def _proj1(xv, mods, mod_row, g, wxbc, wz, wdt, cw, cb, dtb, k):
    bsz, r = xv.shape[:2]
    if k == 1:
        ncol = 1
        x_specs = [pl.BlockSpec((None, r, D), lambda b, i: (b, 0, 0)),
                   pl.BlockSpec((None, HALO, D), lambda b, i: (b, 0, 0)),
                   pl.BlockSpec((None, HALO, D), lambda b, i: (b, 0, 0))]
    else:
        ncol = xv.shape[2]
        nt = ncol // k
        x_specs = [pl.BlockSpec((None, r, k, D), lambda b, i: (b, 0, i, 0)),
                   pl.BlockSpec((None, HALO, k, D), lambda b, i: (b, r // HALO - 1, jnp.maximum(i - 1, 0), 0)),
                   pl.BlockSpec((None, HALO, k, D), lambda b, i: (b, 0, jnp.minimum(i + 1, nt - 1), 0))]
    tm = r * k
    t = r * ncol
    tok = lambda w: pl.BlockSpec((None, tm, w), lambda b, i: (b, i, 0))
    return pl.pallas_call(
        functools.partial(_proj1_kernel, r=r, k=k),
        grid=(bsz, ncol // k),
        in_specs=x_specs + [
                  _mod_spec(mod_row), _const_spec((1, D)),
                  _const_spec((D, SSD_XBC)), _const_spec((D, SSD_INNER)), _const_spec((D, 128)),
                  _const_spec((SSD_CONV, SSD_XBC)), _const_spec((1, SSD_XBC)), _const_spec((1, 128))],
        out_specs=[tok(SSD_XBC), tok(SSD_INNER), tok(128)],
        out_shape=[jax.ShapeDtypeStruct((bsz, t, SSD_XBC), BF16),
                   jax.ShapeDtypeStruct((bsz, t, SSD_INNER), BF16),
                   jax.ShapeDtypeStruct((bsz, t, 128), F32)],
        scratch_shapes=[pltpu.VMEM((tm + 2 * HALO, D), F32), pltpu.VMEM((4, tm + 2 * HALO, 128), F32)],
        compiler_params=_cparams(2),
        name="proj1",
    )(xv, xv, xv, mods, g, wxbc, wz, wdt, cw, cb, dtb)


def _ssd_kernel(*refs, t, with_output):
    if with_output:
        (xs_ref, b_ref, c_ref, dt_ref, z_ref, alog_ref, dexp_ref, ng_ref, s0f_ref, s0b_ref,
         yn_ref, sf_ref, sb_ref, y_acc) = refs
    else:
        xs_ref, b_ref, dt_ref, alog_ref, s0f_ref, s0b_ref, sf_ref, sb_ref = refs
    L = SSD_CHUNK
    n = t // L
    npair = SSD_HPG // 2
    grp = pl.program_id(1)
    row = lax.broadcasted_iota(jnp.int32, (L, L), 0)
    col = lax.broadcasted_iota(jnp.int32, (L, L), 1)
    tril, triu = col <= row, col >= row
    tril_b, triu_b = tril.astype(BF16), triu.astype(BF16)
    lane = lax.broadcasted_iota(jnp.int32, (L, 128), 1)
    lo_half = lane < SSD_P
    a_row = -jnp.exp(alog_ref[...]) * LOG2E

    sf_ref[...] = s0f_ref[...]
    sb_ref[...] = s0b_ref[...]

    def one(rows, d, tri, tri_b, last, s_ref, finish):
        dtc = dt_ref[rows, :]
        acum_all = _dot01_l(tri_b, dtc * a_row)
        yield
        shift = lax.rem(128 - d * SSD_H - grp * SSD_HPG, 128)
        acum = pltpu.roll(acum_all, shift, axis=1)
        dtg = pltpu.roll(dtc, shift, axis=1)
        a_last = acum[last:last + 1, :]
        w = jnp.exp2(a_last - acum) * dtg
        dec = jnp.exp2(a_last)
        pack = jnp.where(lane < SSD_HPG, acum - jnp.log2(dtg), pltpu.roll(w, SSD_HPG, axis=1))
        pack_t = pack.T
        bc = b_ref[rows, :]
        bt = bc.astype(F32).T
        xs = xs_ref[rows, :]
        if with_output:
            cc = c_ref[rows, :]
            ccf = cc.astype(F32)
            cb = _dot_nt(cc, bc)
        ys = []
        for hp in range(npair):
            yield
            heads = (2 * hp, 2 * hp + 1)
            xf = xs[:, hp * 128:(hp + 1) * 128].astype(F32)
            xhalf = [jnp.where(lo_half, xf, 0.0).astype(BF16), jnp.where(lo_half, 0.0, xf).astype(BF16)]
            s_pair = s_ref[hp]
            ds = [_dot((bt * pack_t[SSD_HPG + hh:SSD_HPG + hh + 1, :]).astype(BF16), xhalf[j])
                  for j, hh in enumerate(heads)]
            decs = [jnp.broadcast_to(dec[:, hh:hh + 1], (SSD_N, 128)) for hh in heads]
            s_ref[hp] = s_pair * jnp.concatenate(decs, axis=0) + jnp.concatenate(ds, axis=0)
            if not with_output:
                continue
            lhs, clhs = [], []
            for hh in heads:
                acol = jnp.broadcast_to(acum[:, hh:hh + 1], (L, L))
                m = cb * jnp.exp2(jnp.where(tri, acol - pack_t[hh:hh + 1, :], -1e30))
                lhs.append(m.astype(BF16))
                clhs.append((ccf * jnp.exp2(acol)).astype(BF16))
            rhs = jnp.concatenate(xhalf + [s_pair.astype(BF16)], axis=0)
            ys.append(_dot(jnp.concatenate(lhs + clhs, axis=1), rhs))
        if not with_output:
            return
        y = jnp.concatenate(ys, axis=1)
        if not finish:
            y_acc[rows, :] = y
            return
        y = y + y_acc[rows, :] + dexp_ref[...] * xs.astype(F32)
        y = y * _silu(z_ref[rows, :].astype(F32))
        yn_ref[rows, :] = _rms(y, ng_ref[...]).astype(BF16)

    def body(finish, i, carry):
        fwd = one(pl.ds(pl.multiple_of(i * L, L), L), 0, tril, tril_b, L - 1, sf_ref, finish)
        bwd = one(pl.ds(pl.multiple_of((n - 1 - i) * L, L), L), 1, triu, triu_b, 0, sb_ref, finish)
        _lockstep(fwd, bwd)
        return carry

    lax.fori_loop(0, n // 2, functools.partial(body, False), 0)
    lax.fori_loop(n // 2, n, functools.partial(body, True), 0)


def _ssd(xbc, dt, z, alog, dexp, ng, s0f, s0b, with_output):
    bsz, t, _ = xbc.shape
    assert (t // SSD_CHUNK) % 2 == 0
    gw = SSD_HPG * SSD_P
    st_spec = pl.BlockSpec((None, None, SSD_HPG // 2, 2 * SSD_N, 128), lambda b, g: (b, g, 0, 0, 0))
    st_shape = jax.ShapeDtypeStruct((bsz, SSD_G, SSD_HPG // 2, 2 * SSD_N, 128), F32)
    xs_spec = pl.BlockSpec((None, t, gw), lambda b, g: (b, 0, g))
    b_spec = pl.BlockSpec((None, t, SSD_N), lambda b, g: (b, 0, SSD_INNER // SSD_N + g))
    c_spec = pl.BlockSpec((None, t, SSD_N), lambda b, g: (b, 0, (SSD_INNER + SSD_GS) // SSD_N + g))
    dt_spec = pl.BlockSpec((None, t, 128), lambda b, g: (b, 0, 0))
    row128 = pl.BlockSpec((1, 128), lambda b, g: (0, 0))
    grow = pl.BlockSpec((1, gw), lambda b, g: (0, g))
    if with_output:
        in_specs = [xs_spec, b_spec, c_spec, dt_spec, xs_spec, row128, grow, grow, st_spec, st_spec]
        args = (xbc, xbc, xbc, dt, z, alog, dexp, ng, s0f, s0b)
        out_specs = [xs_spec, st_spec, st_spec]
        out_shape = [jax.ShapeDtypeStruct((bsz, t, SSD_INNER), BF16), st_shape, st_shape]
        scratch = [pltpu.VMEM((t, gw), F32)]
    else:
        in_specs = [xs_spec, b_spec, dt_spec, row128, st_spec, st_spec]
        args = (xbc, xbc, dt, alog, s0f, s0b)
        out_specs = [st_spec, st_spec]
        out_shape = [st_shape, st_shape]
        scratch = []
    return pl.pallas_call(
        functools.partial(_ssd_kernel, t=t, with_output=with_output),
        grid=(bsz, SSD_G),
        in_specs=in_specs, out_specs=out_specs, out_shape=out_shape, scratch_shapes=scratch,
        compiler_params=_cparams(2),
        name="ssd" if with_output else "ssd_state",
    )(*args)


def _tile(t, pref):
    return pref if t % pref == 0 else t


def kernel(x, c, ctx, c_ctx, mod_w, mod_b, norm_g, ffn_w_in, ffn_w_out, ab_w_in, ab_gate_w, ab_gate_b,
           ab_gla_norm_g, ab_vnorm_g, ab_spatial_w, ab_spatial_b, ab_w_out, ssd_w_in, ssd_conv_w,
           ssd_conv_b, ssd_dt_bias, ssd_a_log, ssd_d, ssd_norm_g, ssd_w_out, final_norm_g):
    bsz, t, _ = x.shape
    tc = ctx.shape[1]
    assert mod_w.shape[0] == 2 and bsz <= 7 and t % (GRID_W * 8) == 0 and tc % 128 == 0
    ctx_row = bsz

    cc = jnp.zeros((8, D), F32).at[:bsz].set(c).at[ctx_row].set(c_ctx)
    mods = _mods(cc, mod_w, mod_b)
    m0 = mods[0].reshape(8, 1, N_MOD * D)
    m1 = mods[1].reshape(8, 1, N_MOD * D)

    w = ab_w_in[0]
    o_k, o_v, o_lf, o_lb, o_q, o_r, o_u, o_g = 0, 256, 768, 784, 800, 1056, 1568, 2080
    w0 = jnp.concatenate([w[:, o_v:o_v + 512], w[:, o_r:o_r + 512], w[:, o_u:o_u + 512],
                          w[:, o_g:o_g + 512], w[:, o_k:o_k + 256], w[:, o_q:o_q + 256],
                          w[:, o_lf:o_lf + 32], jnp.zeros((D, 96), F32)], axis=1).astype(BF16)
    gw = jnp.zeros((128, 512), F32)
    gw = gw.at[0:GLA_LR, 0:256].set(ab_gate_w[0, 0]).at[GLA_LR:2 * GLA_LR, 256:512].set(ab_gate_w[0, 1])
    gw = gw.astype(BF16)
    gb = ab_gate_b[0].reshape(1, 512)
    gng = ab_gla_norm_g[0].reshape(1, 512)
    vng = ab_vnorm_g[0].reshape(1, 512)
    sw = ab_spatial_w[0].astype(BF16)
    sbt = ab_spatial_b[0].T
    wo0 = ab_w_out[0].astype(BF16)
    n1g = [norm_g[i, 0].reshape(1, D) for i in range(2)]
    n2g = [norm_g[i, 1].reshape(1, D) for i in range(2)]
    wi = [ffn_w_in[i].astype(BF16) for i in range(2)]
    wo2 = [ffn_w_out[i].astype(BF16) for i in range(2)]

    zeros_gla = jnp.zeros((bsz, GLA_H // 2, 256, 128), F32)
    tmc = _tile(tc, 256)
    pc, lac = _proj0(ctx, m0, ctx_row, n1g[0], w0, gw, gb, tmc)
    ac, sfc, sbc = _gla(pc, lac, zeros_gla, zeros_gla, gng)
    ctx1 = _post0(ctx, m0, ctx_row, ac, pc, vng, sw, sbt, wo0, n2g[0], wi[0], wo2[0], tmc)

    tmx = _tile(t, 512)
    px, lax_ = _proj0(x, m0, None, n1g[0], w0, gw, gb, tmx)
    ax, _, _ = _gla(px, lax_, sfc, sbc, gng)
    x1 = _post0(x, m0, None, ax, px, vng, sw, sbt, wo0, n2g[0], wi[0], wo2[0], tmx)

    w = ssd_w_in[0]
    wxbc = w[:, :SSD_XBC].astype(BF16)
    wdt = jnp.concatenate([w[:, SSD_XBC:SSD_XBC + 2 * SSD_H], jnp.zeros((D, 64), F32)], axis=1).astype(BF16)
    wz = w[:, SSD_XBC + 2 * SSD_H:].astype(BF16)
    cw = ssd_conv_w[0]
    cb = ssd_conv_b[0].reshape(1, SSD_XBC)
    dtb = jnp.concatenate([ssd_dt_bias[0].reshape(1, 2 * SSD_H), jnp.zeros((1, 64), F32)], axis=1)
    alog = jnp.concatenate([ssd_a_log[0].reshape(1, 2 * SSD_H), jnp.zeros((1, 64), F32)], axis=1)
    dexp = jnp.repeat(ssd_d[0], SSD_P).reshape(1, SSD_INNER)
    sng = ssd_norm_g[0].reshape(1, SSD_INNER)
    wo1 = ssd_w_out[0].astype(BF16)

    zeros_ssd = jnp.zeros((bsz, SSD_G, SSD_HPG // 2, 2 * SSD_N, 128), F32)
    xbc_c, _, dt_c = _proj1(ctx1, m1, ctx_row, n1g[1], wxbc, wz, wdt, cw, cb, dtb, 1)
    sfc, sbc = _ssd(xbc_c, dt_c, None, alog, dexp, sng, zeros_ssd, zeros_ssd, False)

    rows = t // GRID_W
    xv = x1.reshape(bsz, rows, GRID_W, D)
    kcol = 8
    xbc_x, z_x, dt_x = _proj1(xv, m1, None, n1g[1], wxbc, wz, wdt, cw, cb, dtb, kcol)
    yn, _, _ = _ssd(xbc_x, dt_x, z_x, alog, dexp, sng, sfc, sbc, True)
    out = _post1(xv, m1, yn, wo1, n2g[1], wi[1], wo2[1], final_norm_g.reshape(1, D), kcol)
    return out.reshape(bsz, t, D)
```

```python
import functools
import itertools

import jax
import jax.numpy as jnp
from jax import lax
from jax.experimental import pallas as pl
from jax.experimental.pallas import tpu as pltpu

F32 = jnp.float32
BF16 = jnp.bfloat16

D = 1024
N_MOD = 6
EPS = 1e-6
LOG2E = 1.4426950408889634
D_FF = 2816
GRID_W = 64

GLA_H, GLA_DK, GLA_DV, GLA_LR, GLA_CHUNK = 4, 64, 128, 16, 64
GLA_TAU_INV = 1.0 / 16.0
GMLP_W, GMLP_G, GMLP_CHUNK = 512, 4, 128
P0_COLS = 2560
P0_W = P0_COLS + 128

SSD_INNER, SSD_P, SSD_H, SSD_G, SSD_HPG, SSD_N, SSD_CHUNK, SSD_CONV = 2048, 64, 32, 4, 8, 128, 128, 5
SSD_GS = SSD_G * SSD_N
SSD_XBC = SSD_INNER + 2 * SSD_GS
HALO = 8
TOKEN_TILE = 512

VMEM_LIMIT = 56 * 1024 * 1024


def _cparams(n_axes):
    return pltpu.CompilerParams(dimension_semantics=("arbitrary",) * n_axes,
                                vmem_limit_bytes=VMEM_LIMIT)


def _dot(a, b):
    return jnp.dot(a, b, preferred_element_type=F32)


def _dot_nt(a, b):
    return lax.dot_general(a, b, (((1,), (1,)), ((), ())), preferred_element_type=F32)


def _dot_tn(a, b):
    return lax.dot_general(a, b, (((0,), (0,)), ((), ())), preferred_element_type=F32)


def _split3(x):
    hi = x.astype(BF16)
    r1 = x - hi.astype(F32)
    mid = r1.astype(BF16)
    lo = (r1 - mid.astype(F32)).astype(BF16)
    return hi, mid, lo


def _dot01_l(t01, x):
    hi, mid, lo = _split3(x)
    return (_dot(t01, lo) + _dot(t01, mid)) + _dot(t01, hi)


def _lockstep(*stages):
    for _ in itertools.zip_longest(*stages):
        pass


def _silu(x):
    return x * jax.nn.sigmoid(x)


def _softplus(x):
    return jnp.maximum(x, 0.0) + jnp.log1p(jnp.exp(-jnp.abs(x)))


def _log_sigmoid(x):
    return jnp.minimum(x, 0.0) - jnp.log1p(jnp.exp(-jnp.abs(x)))


def _rms(x, g):
    return x * lax.rsqrt(jnp.mean(x * x, axis=-1, keepdims=True) + EPS) * g


def _norm_mod(x, g, shift, scale):
    return _rms(x, g) * (1.0 + scale) + shift


def _mods_kernel(cc_ref, w_ref, b_ref, o_ref):
    s = _silu(cc_ref[...])
    o_ref[...] = jnp.dot(s, w_ref[...], precision=lax.Precision.HIGHEST,
                         preferred_element_type=F32) + b_ref[...]


def _mods(cc, mod_w, mod_b):
    depth = mod_w.shape[0]
    return pl.pallas_call(
        _mods_kernel,
        grid=(depth, N_MOD),
        in_specs=[pl.BlockSpec((8, D), lambda i, j: (0, 0)),
                  pl.BlockSpec((None, D, D), lambda i, j: (i, 0, j)),
                  pl.BlockSpec((None, 1, D), lambda i, j: (i, 0, j))],
        out_specs=pl.BlockSpec((None, 8, D), lambda i, j: (i, 0, j)),
        out_shape=jax.ShapeDtypeStruct((depth, 8, N_MOD * D), F32),
        compiler_params=_cparams(2),
        name="mods",
    )(cc, mod_w, mod_b.reshape(depth, 1, N_MOD * D))


def _mod_spec(mod_row):
    if mod_row is None:
        return pl.BlockSpec((None, 1, N_MOD * D), lambda b, i: (b, 0, 0))
    return pl.BlockSpec((None, 1, N_MOD * D), lambda b, i: (mod_row, 0, 0))


def _const_spec(shape):
    nd = len(shape)
    return pl.BlockSpec(shape, lambda *_: (0,) * nd, pipeline_mode=pl.Buffered(1))


def _proj0_kernel(x_ref, mod_ref, g_ref, w_ref, gw_ref, gb_ref, p_ref, la_ref):
    h = _norm_mod(x_ref[...], g_ref[...], mod_ref[:, 0:D], mod_ref[:, D:2 * D]).astype(BF16)
    for c0 in range(0, 2048, 512):
        p_ref[:, c0:c0 + 512] = _dot(h, w_ref[:, c0:c0 + 512]).astype(BF16)
    p_ref[:, 2048:2304] = _dot(h, w_ref[:, 2048:2304]).astype(BF16)
    p_ref[:, 2304:2560] = (_dot(h, w_ref[:, 2304:2560]) * (GLA_DK ** -0.5)).astype(BF16)
    lr = _dot(h, w_ref[:, P0_COLS:P0_W])
    z = _dot(lr.astype(BF16), gw_ref[...]) + gb_ref[...]
    la_ref[...] = _log_sigmoid(z) * GLA_TAU_INV


def _proj0(x, mods, mod_row, g, w, gw, gb, tm):
    bsz, t, _ = x.shape
    return pl.pallas_call(
        _proj0_kernel,
        grid=(bsz, t // tm),
        in_specs=[pl.BlockSpec((None, tm, D), lambda b, i: (b, i, 0)),
                  _mod_spec(mod_row),
                  _const_spec((1, D)),
                  _const_spec((D, P0_W)),
                  _const_spec((128, 512)),
                  _const_spec((1, 512))],
        out_specs=[pl.BlockSpec((None, tm, P0_COLS), lambda b, i: (b, i, 0)),
                   pl.BlockSpec((None, tm, 512), lambda b, i: (b, i, 0))],
        out_shape=[jax.ShapeDtypeStruct((bsz, t, P0_COLS), BF16),
                   jax.ShapeDtypeStruct((bsz, t, 512), F32)],
        compiler_params=_cparams(2),
        name="proj0",
    )(x, mods, g, w, gw, gb)


def _gla_kernel(k_ref, q_ref, v_ref, r_ref, laf_ref, lab_ref, s0f_ref, s0b_ref, gng_ref,
                a_ref, sf_ref, sb_ref, o_acc, *, t):
    L = GLA_CHUNK
    n = t // L
    row = lax.broadcasted_iota(jnp.int32, (L, L), 0)
    col = lax.broadcasted_iota(jnp.int32, (L, L), 1)
    tril, triu = col <= row, col >= row
    tril_b, triu_b = tril.astype(BF16), triu.astype(BF16)
    bd_mask = (lax.broadcasted_iota(jnp.int32, (256, 128), 0) // GLA_DV
               == lax.broadcasted_iota(jnp.int32, (256, 128), 1) // GLA_DK)
    lane = lax.broadcasted_iota(jnp.int32, (L, 128), 1)
    head_masks = (lane < GLA_DK, lane >= GLA_DK)

    sf_ref[...] = s0f_ref[...]
    sb_ref[...] = s0b_ref[...]

    def one(rows, la_ref, tri, tri_b, last, s_ref, finish):
        k = k_ref[rows, :].astype(F32)
        q = q_ref[rows, :].astype(F32)
        v = v_ref[rows, :]
        b = _dot01_l(tri_b, la_ref[rows, :])
        yield
        b_last = b[last:last + 1, :]
        qd = q * jnp.exp(b)
        ki = (k * jnp.exp(-b)).astype(BF16)
        kd = (k * jnp.exp(b_last - b)).astype(BF16)
        sc = [_dot_nt(jnp.where(head_masks[h], qd, 0.0).astype(BF16), ki) for h in range(2)]
        s_prev = s_ref[...]
        o = _dot_nt(qd.astype(BF16), s_prev.astype(BF16))
        s_ref[...] = s_prev * jnp.exp(b_last) + jnp.where(bd_mask, _dot_tn(v, kd), 0.0)
        yield
        intra = [_dot(jnp.where(tri, sc[h], 0.0).astype(BF16), v[:, h * GLA_DV:(h + 1) * GLA_DV])
                 for h in range(2)]
        yield
        o = o + jnp.concatenate(intra, axis=1)
        if not finish:
            o_acc[rows, :] = o
            return
        o = o + o_acc[rows, :]
        r = r_ref[rows, :].astype(F32)
        outs = []
        for h in range(2):
            sl = slice(h * GLA_DV, (h + 1) * GLA_DV)
            outs.append(_rms(o[:, sl], gng_ref[:, sl]) * _silu(r[:, sl]))
        a_ref[rows, :] = jnp.concatenate(outs, axis=1).astype(BF16)

    unroll = 4 if n % 8 == 0 else 2

    def body(finish, iu, carry):
        stages = []
        for u in range(unroll):
            i = iu * unroll + u
            stages.append(one(pl.ds(pl.multiple_of(i * L, L), L), laf_ref, tril, tril_b, L - 1, sf_ref,
                              finish))
            stages.append(one(pl.ds(pl.multiple_of((n - 1 - i) * L, L), L), lab_ref, triu, triu_b, 0,
                              sb_ref, finish))
        _lockstep(*stages)
        return carry

    half = n // (2 * unroll)
    lax.fori_loop(0, half, functools.partial(body, False), 0)
    lax.fori_loop(half, 2 * half, functools.partial(body, True), 0)


def _gla(p, la, s0f, s0b, gng):
    bsz, t, _ = p.shape
    assert (t // GLA_CHUNK) % 4 == 0
    npair = GLA_H // 2
    st_spec = pl.BlockSpec((None, None, 256, 128), lambda b, h: (b, h, 0, 0))
    st_shape = jax.ShapeDtypeStruct((bsz, npair, 256, 128), F32)
    return pl.pallas_call(
        functools.partial(_gla_kernel, t=t),
        grid=(bsz, npair),
        in_specs=[pl.BlockSpec((None, t, 128), lambda b, h: (b, 0, 16 + h)),
                  pl.BlockSpec((None, t, 128), lambda b, h: (b, 0, 18 + h)),
                  pl.BlockSpec((None, t, 256), lambda b, h: (b, 0, h)),
                  pl.BlockSpec((None, t, 256), lambda b, h: (b, 0, 2 + h)),
                  pl.BlockSpec((None, t, 128), lambda b, h: (b, 0, h)),
                  pl.BlockSpec((None, t, 128), lambda b, h: (b, 0, 2 + h)),
                  st_spec, st_spec,
                  pl.BlockSpec((1, 256), lambda b, h: (0, h))],
        out_specs=[pl.BlockSpec((None, t, 256), lambda b, h: (b, 0, h)), st_spec, st_spec],
        out_shape=[jax.ShapeDtypeStruct((bsz, t, 512), BF16), st_shape, st_shape],
        scratch_shapes=[pltpu.VMEM((t, 256), F32)],
        compiler_params=_cparams(2),
        name="gla",
    )(p, p, p, p, la, la, s0f, s0b, gng)


def _ffn_tail(x1, mod_ref, n2g_ref, wi_ref, wo2_ref, act_ref):
    h2 = _norm_mod(x1, n2g_ref[...], mod_ref[:, 3 * D:4 * D], mod_ref[:, 4 * D:5 * D]).astype(BF16)
    for c0 in range(0, D_FF, 512):
        c1 = min(c0 + 512, D_FF)
        gate = _dot(h2, wi_ref[:, c0:c1])
        up = _dot(h2, wi_ref[:, D_FF + c0:D_FF + c1])
        act_ref[:, c0:c1] = (_silu(gate) * up).astype(BF16)
    return x1 + mod_ref[:, 5 * D:6 * D] * _dot(act_ref[...], wo2_ref[...])


def _store_tile(o_ref, x, r, k):
    if k == 1:
        o_ref[...] = x
    else:
        for j in range(k):
            o_ref[:, j, :] = x[j * r:(j + 1) * r, :]


def _post0_kernel(x_ref, mod_ref, a_ref, u_ref, gv_ref, vng_ref, sw_ref, sbt_ref, wo_ref,
                  n2g_ref, wi_ref, wo2_ref, o_ref, mix_ref, act_ref, *, r, k):
    tm = r * k
    x = x_ref[...]
    uu = jax.nn.gelu(u_ref[...].astype(F32), approximate=True)
    vv = jax.nn.gelu(gv_ref[...].astype(F32), approximate=True)
    mu = jnp.mean(vv, axis=-1, keepdims=True)
    vc = vv - mu
    vn = (vc * lax.rsqrt(jnp.mean(vc * vc, axis=-1, keepdims=True) + EPS) * vng_ref[...]).astype(BF16)
    mix_ref[:, 0:512] = a_ref[...]
    for ci in range(tm // GMLP_CHUNK):
        rs = slice(ci * GMLP_CHUNK, (ci + 1) * GMLP_CHUNK)
        for gi in range(GMLP_G):
            cs = slice(gi * 128, (gi + 1) * 128)
            s = _dot(sw_ref[gi], vn[rs, cs]) + sbt_ref[:, gi:gi + 1]
            mix_ref[rs, 512 + gi * 128:512 + (gi + 1) * 128] = (uu[rs, cs] * s).astype(BF16)
    x1 = x + mod_ref[:, 2 * D:3 * D] * _dot(mix_ref[...], wo_ref[...])
    x2 = _ffn_tail(x1, mod_ref, n2g_ref, wi_ref, wo2_ref, act_ref)
    _store_tile(o_ref, x2, r, k)


def _post1_kernel(x_ref, mod_ref, yn_ref, wo_ref, n2g_ref, wi_ref, wo2_ref, fg_ref,
                  o_ref, act_ref, *, r, k):
    x = x_ref[...]
    x1 = x + mod_ref[:, 2 * D:3 * D] * _dot(yn_ref[...], wo_ref[...])
    x2 = _ffn_tail(x1, mod_ref, n2g_ref, wi_ref, wo2_ref, act_ref)
    _store_tile(o_ref, _rms(x2, fg_ref[...]), r, k)


def _post0(x, mods, mod_row, a, p, vng, sw, sbt, wo, n2g, wi, wo2, tm, transpose_grid):
    bsz, t, _ = x.shape
    tok = lambda w, j: pl.BlockSpec((None, tm, w), lambda b, i: (b, i, j))
    if transpose_grid:
        k = tm // GRID_W
        out_spec = pl.BlockSpec((None, GRID_W, k, D), lambda b, i: (b, 0, i, 0))
        out_shape = jax.ShapeDtypeStruct((bsz, GRID_W, t // GRID_W, D), F32)
        kern = functools.partial(_post0_kernel, r=GRID_W, k=k)
    else:
        out_spec, out_shape = tok(D, 0), jax.ShapeDtypeStruct((bsz, t, D), F32)
        kern = functools.partial(_post0_kernel, r=tm, k=1)
    return pl.pallas_call(
        kern,
        grid=(bsz, t // tm),
        in_specs=[tok(D, 0), _mod_spec(mod_row), tok(512, 0), tok(512, 2), tok(512, 3),
                  _const_spec((1, 512)), _const_spec((GMLP_G, 128, 128)), _const_spec((128, GMLP_G)),
                  _const_spec((D, D)), _const_spec((1, D)),
                  _const_spec((D, 2 * D_FF)), _const_spec((D_FF, D))],
        out_specs=out_spec,
        out_shape=out_shape,
        scratch_shapes=[pltpu.VMEM((tm, D), BF16), pltpu.VMEM((tm, D_FF), BF16)],
        compiler_params=_cparams(2),
        name="post0",
    )(x, mods, a, p, p, vng, sw, sbt, wo, n2g, wi, wo2)


def _post1(x, mods, yn, wo, n2g, wi, wo2, fg, tm):
    bsz, t, _ = x.shape
    r = t // GRID_W
    k = tm // r
    tok = lambda w: pl.BlockSpec((None, tm, w), lambda b, i: (b, i, 0))
    return pl.pallas_call(
        functools.partial(_post1_kernel, r=r, k=k),
        grid=(bsz, t // tm),
        in_specs=[tok(D), _mod_spec(None), tok(SSD_INNER),
                  _const_spec((SSD_INNER, D)), _const_spec((1, D)),
                  _const_spec((D, 2 * D_FF)), _const_spec((D_FF, D)), _const_spec((1, D))],
        out_specs=pl.BlockSpec((None, r, k, D), lambda b, i: (b, 0, i, 0)),
        out_shape=jax.ShapeDtypeStruct((bsz, r, GRID_W, D), F32),
        scratch_shapes=[pltpu.VMEM((tm, D_FF), BF16)],
        compiler_params=_cparams(2),
        name="post1",
    )(x, mods, yn, wo, n2g, wi, wo2, fg)


def _proj1_kernel(xm_ref, xp_ref, xn_ref, mod_ref, g_ref, wxbc_ref, wz_ref, wdt_ref,
                  cw_ref, cb_ref, dtb_ref, xbc_ref, z_ref, dt_ref, h_ext, p_ext):
    tm = xm_ref.shape[0]
    i = pl.program_id(1)
    last = pl.num_programs(1) - 1
    g = g_ref[...]
    shift, scale = mod_ref[:, 0:D], mod_ref[:, D:2 * D]
    h_ext[0:HALO, :] = jnp.where(i > 0, _norm_mod(xp_ref[...], g, shift, scale), 0.0)
    h_ext[HALO:HALO + tm, :] = _norm_mod(xm_ref[...], g, shift, scale)
    h_ext[HALO + tm:2 * HALO + tm, :] = jnp.where(i < last, _norm_mod(xn_ref[...], g, shift, scale), 0.0)
    hb = h_ext[...].astype(BF16)
    pad = (SSD_CONV - 1) // 2
    nslab = p_ext.shape[0]
    cwid = nslab * 128
    for c0 in range(0, SSD_XBC, cwid):
        res = _dot(hb, wxbc_ref[:, c0:c0 + cwid])
        for s in range(nslab):
            p_ext[s] = res[:, s * 128:(s + 1) * 128]
        for s in range(nslab):
            cs = slice(c0 + s * 128, c0 + (s + 1) * 128)
            acc = cb_ref[:, cs] + cw_ref[0:1, cs] * p_ext[s, HALO - pad:HALO - pad + tm, :]
            for j in range(1, SSD_CONV):
                acc = acc + cw_ref[j:j + 1, cs] * p_ext[s, HALO - pad + j:HALO - pad + j + tm, :]
            xbc_ref[:, cs] = _silu(acc).astype(BF16)
    hm = h_ext[HALO:HALO + tm, :].astype(BF16)
    for c0 in range(0, SSD_INNER, 512):
        z_ref[:, c0:c0 + 512] = _dot(hm, wz_ref[:, c0:c0 + 512]).astype(BF16)
    dt_ref[...] = _softplus(_dot(hm, wdt_ref[...]) + dtb_ref[...])


def _proj1(x, mods, mod_row, g, wxbc, wz, wdt, cw, cb, dtb, tm):
    bsz, t, _ = x.shape
    hpt = tm // HALO
    tok = lambda w: pl.BlockSpec((None, tm, w), lambda b, i: (b, i, 0))
    return pl.pallas_call(
        _proj1_kernel,
        grid=(bsz, t // tm),
        in_specs=[tok(D),
                  pl.BlockSpec((None, HALO, D), lambda b, i: (b, jnp.maximum(i * hpt - 1, 0), 0)),
                  pl.BlockSpec((None, HALO, D), lambda b, i: (b, jnp.minimum((i + 1) * hpt, t // HALO - 1), 0)),
                  _mod_spec(mod_row), _const_spec((1, D)),
                  _const_spec((D, SSD_XBC)), _const_spec((D, SSD_INNER)), _const_spec((D, 128)),
                  _const_spec((SSD_CONV, SSD_XBC)), _const_spec((1, SSD_XBC)), _const_spec((1, 128))],
        out_specs=[tok(SSD_XBC), tok(SSD_INNER), tok(128)],
        out_shape=[jax.ShapeDtypeStruct((bsz, t, SSD_XBC), BF16),
                   jax.ShapeDtypeStruct((bsz, t, SSD_INNER), BF16),
                   jax.ShapeDtypeStruct((bsz, t, 128), F32)],
        scratch_shapes=[pltpu.VMEM((tm + 2 * HALO, D), F32), pltpu.VMEM((4, tm + 2 * HALO, 128), F32)],
        compiler_params=_cparams(2),
        name="proj1",
    )(x, x, x, mods, g, wxbc, wz, wdt, cw, cb, dtb)


def _ssd_kernel(*refs, t, with_output):
    if with_output:
        (xs_ref, b_ref, c_ref, dt_ref, z_ref, alog_ref, dexp_ref, ng_ref, s0f_ref, s0b_ref,
         yn_ref, sf_ref, sb_ref, pre_ref, dec_ref, y_acc) = refs
    else:
        xs_ref, b_ref, dt_ref, alog_ref, s0f_ref, s0b_ref, sf_ref, sb_ref, pre_ref, dec_ref = refs
    L = SSD_CHUNK
    n = t // L
    npair = SSD_HPG // 2
    grp = pl.program_id(1)
    row = lax.broadcasted_iota(jnp.int32, (L, L), 0)
    col = lax.broadcasted_iota(jnp.int32, (L, L), 1)
    tril, triu = col <= row, col >= row
    tril_b, triu_b = tril.astype(BF16), triu.astype(BF16)
    lane = lax.broadcasted_iota(jnp.int32, (L, 128), 1)
    lo_half = lane < SSD_P
    a_row = -jnp.exp(alog_ref[...]) * LOG2E

    sf_ref[...] = s0f_ref[...]
    sb_ref[...] = s0b_ref[...]

    def rows_of(c):
        return pl.ds(pl.multiple_of(c * L, L), L)

    def prepare(c, d, tri_b, last, slot):
        rows = rows_of(c)
        dtc = dt_ref[rows, :]
        acum_all = _dot01_l(tri_b, dtc * a_row)
        yield
        shift = lax.rem(128 - d * SSD_H - grp * SSD_HPG, 128)
        acum = pltpu.roll(acum_all, shift, axis=1)
        dtg = pltpu.roll(dtc, shift, axis=1)
        a_last = acum[last:last + 1, :]
        w = jnp.exp2(a_last - acum) * dtg
        pack = jnp.where(lane < SSD_HPG, acum - jnp.log2(dtg), pltpu.roll(w, SSD_HPG, axis=1))
        bc = b_ref[rows, :]
        pre_ref[slot, d, 0] = acum
        pre_ref[slot, d, 1] = pack.T
        pre_ref[slot, d, 2] = bc.astype(F32).T
        dec_ref[slot, d] = jnp.broadcast_to(jnp.exp2(a_last), (8, 128))
        if with_output:
            pre_ref[slot, d, 3] = _dot_nt(c_ref[rows, :], bc)

    def update(c, d, tri, s_ref, finish, slot):
        rows = rows_of(c)
        acum_ref, pack_ref, bt_ref, cb_ref = (pre_ref.at[slot, d, j] for j in range(4))
        xs = xs_ref[rows, :]
        if with_output:
            ccf = c_ref[rows, :].astype(F32)
        ys = []
        for hp in range(npair):
            if hp:
                yield
            heads = (2 * hp, 2 * hp + 1)
            xf = xs[:, hp * 128:(hp + 1) * 128].astype(F32)
            xhalf = [jnp.where(lo_half, xf, 0.0).astype(BF16), jnp.where(lo_half, 0.0, xf).astype(BF16)]
            s_pair = s_ref[hp]
            ds = [_dot((bt_ref[...] * pack_ref[SSD_HPG + hh:SSD_HPG + hh + 1, :]).astype(BF16), xhalf[j])
                  for j, hh in enumerate(heads)]
            decs = [jnp.broadcast_to(dec_ref[slot, d, 0:1, hh:hh + 1], (SSD_N, 128)) for hh in heads]
            s_ref[hp] = s_pair * jnp.concatenate(decs, axis=0) + jnp.concatenate(ds, axis=0)
            if not with_output:
                continue
            lhs, clhs = [], []
            for hh in heads:
                acol = jnp.broadcast_to(acum_ref[:, hh:hh + 1], (L, L))
                m = cb_ref[...] * jnp.exp2(jnp.where(tri, acol - pack_ref[hh:hh + 1, :], -1e30))
                lhs.append(m.astype(BF16))
                clhs.append((ccf * jnp.exp2(acol)).astype(BF16))
            rhs = jnp.concatenate(xhalf + [s_pair.astype(BF16)], axis=0)
            ys.append(_dot(jnp.concatenate(lhs + clhs, axis=1), rhs))
        if not with_output:
            return
        yield
        y = jnp.concatenate(ys, axis=1)
        if not finish:
            y_acc[rows, :] = y
            return
        y = y + y_acc[rows, :] + dexp_ref[...] * xs.astype(F32)
        y = y * _silu(z_ref[rows, :].astype(F32))
        yn_ref[rows, :] = _rms(y, ng_ref[...]).astype(BF16)

    unr = 2 if n % 4 == 0 else 1

    def prepares(first, bank):
        out = []
        for u in range(unr):
            i = first + u
            out += [prepare(jnp.minimum(i, n - 1), 0, tril_b, L - 1, bank * unr + u),
                    prepare(jnp.maximum(n - 1 - i, 0), 1, triu_b, 0, bank * unr + u)]
        return out

    def body(finish, iu, carry):
        bank = lax.rem(iu, 2)
        ups = []
        for u in range(unr):
            i = iu * unr + u
            ups += [update(i, 0, tril, sf_ref, finish, bank * unr + u),
                    update(n - 1 - i, 1, triu, sb_ref, finish, bank * unr + u)]
        _lockstep(*(ups + prepares((iu + 1) * unr, 1 - bank)))
        return carry

    _lockstep(*prepares(0, 0))
    half = n // (2 * unr)
    lax.fori_loop(0, half, functools.partial(body, False), 0)
    lax.fori_loop(half, 2 * half, functools.partial(body, True), 0)


def _ssd(xbc, dt, z, alog, dexp, ng, s0f, s0b, with_output):
    bsz, t, _ = xbc.shape
    assert (t // SSD_CHUNK) % 2 == 0
    gw = SSD_HPG * SSD_P
    st_spec = pl.BlockSpec((None, None, SSD_HPG // 2, 2 * SSD_N, 128), lambda b, g: (b, g, 0, 0, 0))
    st_shape = jax.ShapeDtypeStruct((bsz, SSD_G, SSD_HPG // 2, 2 * SSD_N, 128), F32)
    xs_spec = pl.BlockSpec((None, t, gw), lambda b, g: (b, 0, g))
    b_spec = pl.BlockSpec((None, t, SSD_N), lambda b, g: (b, 0, SSD_INNER // SSD_N + g))
    c_spec = pl.BlockSpec((None, t, SSD_N), lambda b, g: (b, 0, (SSD_INNER + SSD_GS) // SSD_N + g))
    dt_spec = pl.BlockSpec((None, t, 128), lambda b, g: (b, 0, 0))
    row128 = pl.BlockSpec((1, 128), lambda b, g: (0, 0))
    grow = pl.BlockSpec((1, gw), lambda b, g: (0, g))
    pre = [pltpu.VMEM((4, 2, 4, SSD_CHUNK, 128), F32), pltpu.VMEM((4, 2, 8, 128), F32)]
    if with_output:
        in_specs = [xs_spec, b_spec, c_spec, dt_spec, xs_spec, row128, grow, grow, st_spec, st_spec]
        args = (xbc, xbc, xbc, dt, z, alog, dexp, ng, s0f, s0b)
        out_specs = [xs_spec, st_spec, st_spec]
        out_shape = [jax.ShapeDtypeStruct((bsz, t, SSD_INNER), BF16), st_shape, st_shape]
        scratch = pre + [pltpu.VMEM((t, gw), F32)]
    else:
        in_specs = [xs_spec, b_spec, dt_spec, row128, st_spec, st_spec]
        args = (xbc, xbc, dt, alog, s0f, s0b)
        out_specs = [st_spec, st_spec]
        out_shape = [st_shape, st_shape]
        scratch = pre
    return pl.pallas_call(
        functools.partial(_ssd_kernel, t=t, with_output=with_output),
        grid=(bsz, SSD_G),
        in_specs=in_specs, out_specs=out_specs, out_shape=out_shape, scratch_shapes=scratch,
        compiler_params=_cparams(2),
        name="ssd" if with_output else "ssd_state",
    )(*args)


def _tile(t, pref):
    return pref if t % pref == 0 else t


def kernel(x, c, ctx, c_ctx, mod_w, mod_b, norm_g, ffn_w_in, ffn_w_out, ab_w_in, ab_gate_w, ab_gate_b,
           ab_gla_norm_g, ab_vnorm_g, ab_spatial_w, ab_spatial_b, ab_w_out, ssd_w_in, ssd_conv_w,
           ssd_conv_b, ssd_dt_bias, ssd_a_log, ssd_d, ssd_norm_g, ssd_w_out, final_norm_g):
    bsz, t, _ = x.shape
    tc = ctx.shape[1]
    assert mod_w.shape[0] == 2 and bsz <= 7 and tc % 128 == 0
    assert t % TOKEN_TILE == 0 and TOKEN_TILE % (8 * GRID_W) == 0 and (TOKEN_TILE * GRID_W) % (8 * t) == 0
    ctx_row = bsz

    cc = jnp.zeros((8, D), F32).at[:bsz].set(c).at[ctx_row].set(c_ctx)
    mods = _mods(cc, mod_w, mod_b)
    m0 = mods[0].reshape(8, 1, N_MOD * D)
    m1 = mods[1].reshape(8, 1, N_MOD * D)

    w = ab_w_in[0]
    o_k, o_v, o_lf, o_lb, o_q, o_r, o_u, o_g = 0, 256, 768, 784, 800, 1056, 1568, 2080
    w0 = jnp.concatenate([w[:, o_v:o_v + 512], w[:, o_r:o_r + 512], w[:, o_u:o_u + 512],
                          w[:, o_g:o_g + 512], w[:, o_k:o_k + 256], w[:, o_q:o_q + 256],
                          w[:, o_lf:o_lf + 32], jnp.zeros((D, 96), F32)], axis=1).astype(BF16)
    gw = jnp.zeros((128, 512), F32)
    gw = gw.at[0:GLA_LR, 0:256].set(ab_gate_w[0, 0]).at[GLA_LR:2 * GLA_LR, 256:512].set(ab_gate_w[0, 1])
    gw = gw.astype(BF16)
    gb = ab_gate_b[0].reshape(1, 512)
    gng = ab_gla_norm_g[0].reshape(1, 512)
    vng = ab_vnorm_g[0].reshape(1, 512)
    sw = ab_spatial_w[0].astype(BF16)
    sbt = ab_spatial_b[0].T
    wo0 = ab_w_out[0].astype(BF16)
    n1g = [norm_g[i, 0].reshape(1, D) for i in range(2)]
    n2g = [norm_g[i, 1].reshape(1, D) for i in range(2)]
    wi = [ffn_w_in[i].astype(BF16) for i in range(2)]
    wo2 = [ffn_w_out[i].astype(BF16) for i in range(2)]

    zeros_gla = jnp.zeros((bsz, GLA_H // 2, 256, 128), F32)
    tmc = _tile(tc, 256)
    pc, lac = _proj0(ctx, m0, ctx_row, n1g[0], w0, gw, gb, tmc)
    ac, sfc, sbc = _gla(pc, lac, zeros_gla, zeros_gla, gng)
    ctx1 = _post0(ctx, m0, ctx_row, ac, pc, vng, sw, sbt, wo0, n2g[0], wi[0], wo2[0], tmc, False)

    tmx = TOKEN_TILE
    px, lax_ = _proj0(x, m0, None, n1g[0], w0, gw, gb, tmx)
    ax, _, _ = _gla(px, lax_, sfc, sbc, gng)
    x1 = _post0(x, m0, None, ax, px, vng, sw, sbt, wo0, n2g[0], wi[0], wo2[0], tmx, True)
    x1 = x1.reshape(bsz, t, D)

    w = ssd_w_in[0]
    wxbc = w[:, :SSD_XBC].astype(BF16)
    wdt = jnp.concatenate([w[:, SSD_XBC:SSD_XBC + 2 * SSD_H], jnp.zeros((D, 64), F32)], axis=1).astype(BF16)
    wz = w[:, SSD_XBC + 2 * SSD_H:].astype(BF16)
    cw = ssd_conv_w[0]
    cb = ssd_conv_b[0].reshape(1, SSD_XBC)
    dtb = jnp.concatenate([ssd_dt_bias[0].reshape(1, 2 * SSD_H), jnp.zeros((1, 64), F32)], axis=1)
    alog = jnp.concatenate([ssd_a_log[0].reshape(1, 2 * SSD_H), jnp.zeros((1, 64), F32)], axis=1)
    dexp = jnp.repeat(ssd_d[0], SSD_P).reshape(1, SSD_INNER)
    sng = ssd_norm_g[0].reshape(1, SSD_INNER)
    wo1 = ssd_w_out[0].astype(BF16)

    zeros_ssd = jnp.zeros((bsz, SSD_G, SSD_HPG // 2, 2 * SSD_N, 128), F32)
    xbc_c, _, dt_c = _proj1(ctx1, m1, ctx_row, n1g[1], wxbc, wz, wdt, cw, cb, dtb, tmc)
    sfc, sbc = _ssd(xbc_c, dt_c, None, alog, dexp, sng, zeros_ssd, zeros_ssd, False)

    xbc_x, z_x, dt_x = _proj1(x1, m1, None, n1g[1], wxbc, wz, wdt, cw, cb, dtb, tmx)
    yn, _, _ = _ssd(xbc_x, dt_x, z_x, alog, dexp, sng, sfc, sbc, True)
    out = _post1(x1, m1, yn, wo1, n2g[1], wi[1], wo2[1], final_norm_g.reshape(1, D), tmx)
    return out.reshape(bsz, t, D)
```

```python
import functools
import itertools

import jax
import jax.numpy as jnp
from jax import lax
from jax.experimental import pallas as pl
from jax.experimental.pallas import tpu as pltpu

F32 = jnp.float32
BF16 = jnp.bfloat16

D = 1024
N_MOD = 6
EPS = 1e-6
LOG2E = 1.4426950408889634
D_FF = 2816
GRID_W = 64

GLA_H, GLA_DK, GLA_DV, GLA_LR, GLA_CHUNK = 4, 64, 128, 16, 64
GLA_TAU_INV = 1.0 / 16.0
GMLP_W, GMLP_G, GMLP_CHUNK = 512, 4, 128
P0_COLS = 2560
P0_W = P0_COLS + 128

SSD_INNER, SSD_P, SSD_H, SSD_G, SSD_HPG, SSD_N, SSD_CHUNK, SSD_CONV = 2048, 64, 32, 4, 8, 128, 128, 5
SSD_GS = SSD_G * SSD_N
SSD_XBC = SSD_INNER + 2 * SSD_GS
HALO = 8
TOKEN_TILE = 512

VMEM_LIMIT = 56 * 1024 * 1024


def _cparams(n_axes):
    return pltpu.CompilerParams(dimension_semantics=("arbitrary",) * n_axes,
                                vmem_limit_bytes=VMEM_LIMIT)


def _dot(a, b):
    return jnp.dot(a, b, preferred_element_type=F32)


def _dot_nt(a, b):
    return lax.dot_general(a, b, (((1,), (1,)), ((), ())), preferred_element_type=F32)


def _dot_tn(a, b):
    return lax.dot_general(a, b, (((0,), (0,)), ((), ())), preferred_element_type=F32)


def _split3(x):
    hi = x.astype(BF16)
    r1 = x - hi.astype(F32)
    mid = r1.astype(BF16)
    lo = (r1 - mid.astype(F32)).astype(BF16)
    return hi, mid, lo


def _dot01_l(t01, x):
    hi, mid, lo = _split3(x)
    return (_dot(t01, lo) + _dot(t01, mid)) + _dot(t01, hi)


def _lockstep(*stages):
    for _ in itertools.zip_longest(*stages):
        pass


def _silu(x):
    return x * jax.nn.sigmoid(x)


def _softplus(x):
    return jnp.maximum(x, 0.0) + jnp.log1p(jnp.exp(-jnp.abs(x)))


def _log_sigmoid(x):
    return jnp.minimum(x, 0.0) - jnp.log1p(jnp.exp(-jnp.abs(x)))


def _rms(x, g):
    return x * lax.rsqrt(jnp.mean(x * x, axis=-1, keepdims=True) + EPS) * g


def _norm_mod(x, g, shift, scale):
    return _rms(x, g) * (1.0 + scale) + shift


def _mods_kernel(cc_ref, w_ref, b_ref, o_ref):
    s = _silu(cc_ref[...])
    o_ref[...] = jnp.dot(s, w_ref[...], precision=lax.Precision.HIGHEST,
                         preferred_element_type=F32) + b_ref[...]


def _mods(cc, mod_w, mod_b):
    depth = mod_w.shape[0]
    wb = N_MOD * D // 2
    return pl.pallas_call(
        _mods_kernel,
        grid=(depth, 2),
        in_specs=[pl.BlockSpec((8, D), lambda i, j: (0, 0)),
                  pl.BlockSpec((None, D, wb), lambda i, j: (i, 0, j)),
                  pl.BlockSpec((None, 1, wb), lambda i, j: (i, 0, j))],
        out_specs=pl.BlockSpec((None, 8, wb), lambda i, j: (i, 0, j)),
        out_shape=jax.ShapeDtypeStruct((depth, 8, N_MOD * D), F32),
        compiler_params=_cparams(2),
        name="mods",
    )(cc, mod_w, mod_b.reshape(depth, 1, N_MOD * D))


def _mod_spec(mod_row):
    if mod_row is None:
        return pl.BlockSpec((None, 1, N_MOD * D), lambda b, i: (b, 0, 0))
    return pl.BlockSpec((None, 1, N_MOD * D), lambda b, i: (mod_row, 0, 0))


def _const_spec(shape, index=None):
    index = (0,) * len(shape) if index is None else index
    return pl.BlockSpec(shape, lambda *_: index, pipeline_mode=pl.Buffered(1))


def _ffn_specs(layer):
    return [_const_spec((None, D, 2 * D_FF), (layer, 0, 0)), _const_spec((None, D_FF, D), (layer, 0, 0))]


def _proj0_kernel(x_ref, mod_ref, g_ref, w_ref, gw_ref, gb_ref, p_ref, la_ref):
    h = _norm_mod(x_ref[...], g_ref[...], mod_ref[:, 0:D], mod_ref[:, D:2 * D]).astype(BF16)
    for c0 in range(0, 2048, 512):
        p_ref[:, c0:c0 + 512] = _dot(h, w_ref[:, c0:c0 + 512]).astype(BF16)
    p_ref[:, 2048:2304] = _dot(h, w_ref[:, 2048:2304]).astype(BF16)
    p_ref[:, 2304:2560] = (_dot(h, w_ref[:, 2304:2560]) * (GLA_DK ** -0.5)).astype(BF16)
    lr = _dot(h, w_ref[:, P0_COLS:P0_W])
    z = _dot(lr.astype(BF16), gw_ref[...]) + gb_ref[...]
    la_ref[...] = _log_sigmoid(z) * GLA_TAU_INV


def _proj0(x, mods, mod_row, g, w, gw, gb, tm):
    bsz, t, _ = x.shape
    return pl.pallas_call(
        _proj0_kernel,
        grid=(bsz, t // tm),
        in_specs=[pl.BlockSpec((None, tm, D), lambda b, i: (b, i, 0)),
                  _mod_spec(mod_row),
                  _const_spec((1, D)),
                  _const_spec((D, P0_W)),
                  _const_spec((128, 512)),
                  _const_spec((1, 512))],
        out_specs=[pl.BlockSpec((None, tm, P0_COLS), lambda b, i: (b, i, 0)),
                   pl.BlockSpec((None, tm, 512), lambda b, i: (b, i, 0))],
        out_shape=[jax.ShapeDtypeStruct((bsz, t, P0_COLS), BF16),
                   jax.ShapeDtypeStruct((bsz, t, 512), F32)],
        compiler_params=_cparams(2),
        name="proj0",
    )(x, mods, g, w, gw, gb)


def _gla_kernel(*refs):
    ctx_in, lat_in, gng_ref = refs[0:6], refs[6:12], refs[12]
    ctx_a, lat_a, ctx_acc, lat_acc, sf_ref, sb_ref = refs[13:19]
    L = GLA_CHUNK
    row = lax.broadcasted_iota(jnp.int32, (L, L), 0)
    col = lax.broadcasted_iota(jnp.int32, (L, L), 1)
    tril, triu = col <= row, col >= row
    tril_b, triu_b = tril.astype(BF16), triu.astype(BF16)
    bd_mask = (lax.broadcasted_iota(jnp.int32, (256, 128), 0) // GLA_DV
               == lax.broadcasted_iota(jnp.int32, (256, 128), 1) // GLA_DK)
    lane = lax.broadcasted_iota(jnp.int32, (L, 128), 1)
    head_masks = (lane < GLA_DK, lane >= GLA_DK)

    sf_ref[...] = jnp.zeros_like(sf_ref)
    sb_ref[...] = jnp.zeros_like(sb_ref)

    def scan(k_ref, q_ref, v_ref, r_ref, laf_ref, lab_ref, a_ref, o_acc):
        n = k_ref.shape[0] // L

        def one(rows, la_ref, tri, tri_b, last, s_ref, finish):
            k = k_ref[rows, :].astype(F32)
            q = q_ref[rows, :].astype(F32)
            v = v_ref[rows, :]
            b = _dot01_l(tri_b, la_ref[rows, :])
            yield
            b_last = b[last:last + 1, :]
            qd = q * jnp.exp(b)
            ki = (k * jnp.exp(-b)).astype(BF16)
            kd = (k * jnp.exp(b_last - b)).astype(BF16)
            sc = [_dot_nt(jnp.where(head_masks[h], qd, 0.0).astype(BF16), ki) for h in range(2)]
            s_prev = s_ref[...]
            o = _dot_nt(qd.astype(BF16), s_prev.astype(BF16))
            s_ref[...] = s_prev * jnp.exp(b_last) + jnp.where(bd_mask, _dot_tn(v, kd), 0.0)
            yield
            intra = [_dot(jnp.where(tri, sc[h], 0.0).astype(BF16), v[:, h * GLA_DV:(h + 1) * GLA_DV])
                     for h in range(2)]
            yield
            o = o + jnp.concatenate(intra, axis=1)
            if not finish:
                o_acc[rows, :] = o
                return
            o = o + o_acc[rows, :]
            r = r_ref[rows, :].astype(F32)
            outs = []
            for h in range(2):
                sl = slice(h * GLA_DV, (h + 1) * GLA_DV)
                outs.append(_rms(o[:, sl], gng_ref[:, sl]) * _silu(r[:, sl]))
            a_ref[rows, :] = jnp.concatenate(outs, axis=1).astype(BF16)

        unroll = 4 if n % 8 == 0 else 2

        def body(finish, iu, carry):
            stages = []
            for u in range(unroll):
                i = iu * unroll + u
                stages.append(one(pl.ds(pl.multiple_of(i * L, L), L), laf_ref, tril, tril_b, L - 1, sf_ref,
                                  finish))
                stages.append(one(pl.ds(pl.multiple_of((n - 1 - i) * L, L), L), lab_ref, triu, triu_b, 0,
                                  sb_ref, finish))
            _lockstep(*stages)
            return carry

        half = n // (2 * unroll)
        lax.fori_loop(0, half, functools.partial(body, False), 0)
        lax.fori_loop(half, 2 * half, functools.partial(body, True), 0)

    scan(*ctx_in, ctx_a, ctx_acc)
    scan(*lat_in, lat_a, lat_acc)


def _gla(pc, lac, p, la, gng):
    bsz, t, _ = p.shape
    tc = pc.shape[1]
    assert (t // GLA_CHUNK) % 4 == 0 and (tc // GLA_CHUNK) % 4 == 0
    npair = GLA_H // 2
    kb, qb = 2048 // 128, 2304 // 128

    def operands(n):
        tok = lambda w, f: pl.BlockSpec((None, n, w), lambda b, h: (b, 0, f(h)))
        return [tok(128, lambda h: kb + h), tok(128, lambda h: qb + h),
                tok(256, lambda h: h), tok(256, lambda h: npair + h),
                tok(128, lambda h: h), tok(128, lambda h: npair + h)]

    out = lambda n: pl.BlockSpec((None, n, 256), lambda b, h: (b, 0, h))
    return pl.pallas_call(
        _gla_kernel,
        grid=(bsz, npair),
        in_specs=operands(tc) + operands(t) + [pl.BlockSpec((1, 256), lambda b, h: (0, h))],
        out_specs=[out(tc), out(t)],
        out_shape=[jax.ShapeDtypeStruct((bsz, tc, 512), BF16), jax.ShapeDtypeStruct((bsz, t, 512), BF16)],
        scratch_shapes=[pltpu.VMEM((tc, 256), F32), pltpu.VMEM((t, 256), F32),
                        pltpu.VMEM((256, 128), F32), pltpu.VMEM((256, 128), F32)],
        compiler_params=_cparams(2),
        name="gla",
    )(pc, pc, pc, pc, lac, lac, p, p, p, p, la, la, gng)


def _ffn_tail(x1, mod_ref, n2g_ref, wi_ref, wo2_ref, act_ref):
    h2 = _norm_mod(x1, n2g_ref[...], mod_ref[:, 3 * D:4 * D], mod_ref[:, 4 * D:5 * D]).astype(BF16)
    for c0 in range(0, D_FF, 512):
        c1 = min(c0 + 512, D_FF)
        gate = _dot(h2, wi_ref[:, c0:c1])
        up = _dot(h2, wi_ref[:, D_FF + c0:D_FF + c1])
        act_ref[:, c0:c1] = (_silu(gate) * up).astype(BF16)
    return x1 + mod_ref[:, 5 * D:6 * D] * _dot(act_ref[...], wo2_ref[...])


def _store_tile(o_ref, x, r, k):
    if k == 1:
        o_ref[...] = x
    else:
        for j in range(k):
            o_ref[:, j, :] = x[j * r:(j + 1) * r, :]


def _post0_kernel(x_ref, mod_ref, a_ref, u_ref, gv_ref, vng_ref, sw_ref, sbt_ref, wo_ref,
                  n2g_ref, wi_ref, wo2_ref, o_ref, mix_ref, act_ref, *, r, k):
    tm = r * k
    x = x_ref[...]
    uu = jax.nn.gelu(u_ref[...].astype(F32), approximate=True)
    vv = jax.nn.gelu(gv_ref[...].astype(F32), approximate=True)
    mu = jnp.mean(vv, axis=-1, keepdims=True)
    vc = vv - mu
    vn = (vc * lax.rsqrt(jnp.mean(vc * vc, axis=-1, keepdims=True) + EPS) * vng_ref[...]).astype(BF16)
    mix_ref[:, 0:512] = a_ref[...]
    for ci in range(tm // GMLP_CHUNK):
        rs = slice(ci * GMLP_CHUNK, (ci + 1) * GMLP_CHUNK)
        for gi in range(GMLP_G):
            cs = slice(gi * 128, (gi + 1) * 128)
            s = _dot(sw_ref[gi], vn[rs, cs]) + sbt_ref[:, gi:gi + 1]
            mix_ref[rs, 512 + gi * 128:512 + (gi + 1) * 128] = (uu[rs, cs] * s).astype(BF16)
    x1 = x + mod_ref[:, 2 * D:3 * D] * _dot(mix_ref[...], wo_ref[...])
    x2 = _ffn_tail(x1, mod_ref, n2g_ref, wi_ref, wo2_ref, act_ref)
    _store_tile(o_ref, x2, r, k)


def _post1_kernel(x_ref, mod_ref, yn_ref, wo_ref, n2g_ref, wi_ref, wo2_ref, fg_ref,
                  o_ref, act_ref, *, r, k):
    x = x_ref[...]
    x1 = x + mod_ref[:, 2 * D:3 * D] * _dot(yn_ref[...], wo_ref[...])
    x2 = _ffn_tail(x1, mod_ref, n2g_ref, wi_ref, wo2_ref, act_ref)
    _store_tile(o_ref, _rms(x2, fg_ref[...]), r, k)


def _post0(x, mods, mod_row, a, p, vng, sw, sbt, wo, n2g, wi, wo2, layer, tm, transpose_grid):
    bsz, t, _ = x.shape
    tok = lambda w, j: pl.BlockSpec((None, tm, w), lambda b, i: (b, i, j))
    if transpose_grid:
        k = tm // GRID_W
        out_spec = pl.BlockSpec((None, GRID_W, k, D), lambda b, i: (b, 0, i, 0))
        out_shape = jax.ShapeDtypeStruct((bsz, GRID_W, t // GRID_W, D), F32)
        kern = functools.partial(_post0_kernel, r=GRID_W, k=k)
    else:
        out_spec, out_shape = tok(D, 0), jax.ShapeDtypeStruct((bsz, t, D), F32)
        kern = functools.partial(_post0_kernel, r=tm, k=1)
    return pl.pallas_call(
        kern,
        grid=(bsz, t // tm),
        in_specs=[tok(D, 0), _mod_spec(mod_row), tok(512, 0), tok(512, 2), tok(512, 3),
                  _const_spec((1, 512)), _const_spec((GMLP_G, 128, 128)), _const_spec((128, GMLP_G)),
                  _const_spec((D, D)), _const_spec((1, D))] + _ffn_specs(layer),
        out_specs=out_spec,
        out_shape=out_shape,
        scratch_shapes=[pltpu.VMEM((tm, D), BF16), pltpu.VMEM((tm, D_FF), BF16)],
        compiler_params=_cparams(2),
        name="post0",
    )(x, mods, a, p, p, vng, sw, sbt, wo, n2g, wi, wo2)


def _post1(x, mods, yn, wo, n2g, wi, wo2, layer, fg, tm):
    bsz, t, _ = x.shape
    r = t // GRID_W
    k = tm // r
    tok = lambda w: pl.BlockSpec((None, tm, w), lambda b, i: (b, i, 0))
    return pl.pallas_call(
        functools.partial(_post1_kernel, r=r, k=k),
        grid=(bsz, t // tm),
        in_specs=[tok(D), _mod_spec(None), tok(SSD_INNER),
                  _const_spec((SSD_INNER, D)), _const_spec((1, D))] + _ffn_specs(layer) + [_const_spec((1, D))],
        out_specs=pl.BlockSpec((None, r, k, D), lambda b, i: (b, 0, i, 0)),
        out_shape=jax.ShapeDtypeStruct((bsz, r, GRID_W, D), F32),
        scratch_shapes=[pltpu.VMEM((tm, D_FF), BF16)],
        compiler_params=_cparams(2),
        name="post1",
    )(x, mods, yn, wo, n2g, wi, wo2, fg)


def _proj1_kernel(*refs, state_only):
    if state_only:
        (xm_ref, xp_ref, xn_ref, mod_ref, g_ref, wxbc_ref, wdt_ref, cw_ref, cb_ref, dtb_ref,
         xbc_ref, dt_ref, h_ext, p_ext) = refs
    else:
        (xm_ref, xp_ref, xn_ref, mod_ref, g_ref, wxbc_ref, wz_ref, wdt_ref, cw_ref, cb_ref, dtb_ref,
         xbc_ref, z_ref, dt_ref, h_ext, p_ext) = refs
    tm = xm_ref.shape[0]
    i = pl.program_id(1)
    last = pl.num_programs(1) - 1
    g = g_ref[...]
    shift, scale = mod_ref[:, 0:D], mod_ref[:, D:2 * D]
    h_ext[0:HALO, :] = jnp.where(i > 0, _norm_mod(xp_ref[...], g, shift, scale), 0.0)
    h_ext[HALO:HALO + tm, :] = _norm_mod(xm_ref[...], g, shift, scale)
    h_ext[HALO + tm:2 * HALO + tm, :] = jnp.where(i < last, _norm_mod(xn_ref[...], g, shift, scale), 0.0)
    hb = h_ext[...].astype(BF16)
    pad = (SSD_CONV - 1) // 2
    nslab = p_ext.shape[0]
    cwid = nslab * 128
    for c0 in range(0, xbc_ref.shape[1], cwid):
        res = _dot(hb, wxbc_ref[:, c0:c0 + cwid])
        for s in range(nslab):
            p_ext[s] = res[:, s * 128:(s + 1) * 128]
        for s in range(nslab):
            cs = slice(c0 + s * 128, c0 + (s + 1) * 128)
            acc = cb_ref[:, cs] + cw_ref[0:1, cs] * p_ext[s, HALO - pad:HALO - pad + tm, :]
            for j in range(1, SSD_CONV):
                acc = acc + cw_ref[j:j + 1, cs] * p_ext[s, HALO - pad + j:HALO - pad + j + tm, :]
            xbc_ref[:, cs] = _silu(acc).astype(BF16)
    hm = h_ext[HALO:HALO + tm, :].astype(BF16)
    if not state_only:
        for c0 in range(0, SSD_INNER, 512):
            z_ref[:, c0:c0 + 512] = _dot(hm, wz_ref[:, c0:c0 + 512]).astype(BF16)
    dt_ref[...] = _softplus(_dot(hm, wdt_ref[...]) + dtb_ref[...])


def _proj1(x, mods, mod_row, g, wall, wz, cw, cb, dtb, tm, state_only):
    bsz, t, _ = x.shape
    hpt = tm // HALO
    tok = lambda w: pl.BlockSpec((None, tm, w), lambda b, i: (b, i, 0))
    shp = lambda w, dt: jax.ShapeDtypeStruct((bsz, t, w), dt)
    ncol = SSD_INNER + SSD_GS if state_only else SSD_XBC
    w_specs = [_const_spec((D, ncol))] + ([] if state_only else [_const_spec((D, SSD_INNER))])
    w_specs += [_const_spec((D, 128), (0, SSD_XBC // 128))]
    w_args = (wall,) + (() if state_only else (wz,)) + (wall,)
    return pl.pallas_call(
        functools.partial(_proj1_kernel, state_only=state_only),
        grid=(bsz, t // tm),
        in_specs=[tok(D),
                  pl.BlockSpec((None, HALO, D), lambda b, i: (b, jnp.maximum(i * hpt - 1, 0), 0)),
                  pl.BlockSpec((None, HALO, D), lambda b, i: (b, jnp.minimum((i + 1) * hpt, t // HALO - 1), 0)),
                  _mod_spec(mod_row), _const_spec((1, D))] + w_specs +
                 [_const_spec((SSD_CONV, ncol)), _const_spec((1, ncol)), _const_spec((1, 128))],
        out_specs=[tok(ncol)] + ([] if state_only else [tok(SSD_INNER)]) + [tok(128)],
        out_shape=[shp(ncol, BF16)] + ([] if state_only else [shp(SSD_INNER, BF16)]) + [shp(128, F32)],
        scratch_shapes=[pltpu.VMEM((tm + 2 * HALO, D), F32), pltpu.VMEM((4, tm + 2 * HALO, 128), F32)],
        compiler_params=_cparams(2),
        name="proj1_state" if state_only else "proj1",
    )(x, x, x, mods, g, *w_args, cw, cb, dtb)


def _ssd_kernel(xs_ref, b_ref, c_ref, dt_ref, z_ref, alog_ref, dexp_ref, ng_ref, cxs_ref, cb_ref_, cdt_ref,
                yn_ref, pre_ref, dec_ref, y_acc, sf_ref, sb_ref):
    L = SSD_CHUNK
    n = xs_ref.shape[0] // L
    nc = cxs_ref.shape[0] // L
    npair = SSD_HPG // 2
    grp = pl.program_id(1)
    row = lax.broadcasted_iota(jnp.int32, (L, L), 0)
    col = lax.broadcasted_iota(jnp.int32, (L, L), 1)
    tril, triu = col <= row, col >= row
    tril_b, triu_b = tril.astype(BF16), triu.astype(BF16)
    lane = lax.broadcasted_iota(jnp.int32, (L, 128), 1)
    lo_half = lane < SSD_P
    a_row = -jnp.exp(alog_ref[...]) * LOG2E

    sf_ref[...] = jnp.zeros_like(sf_ref)
    sb_ref[...] = jnp.zeros_like(sb_ref)
    latent = (xs_ref, b_ref, c_ref, dt_ref)
    context = (cxs_ref, cb_ref_, None, cdt_ref)

    def rows_of(c):
        return pl.ds(c * L if isinstance(c, int) else pl.multiple_of(c * L, L), L)

    def prepare(src, c, d, tri_b, last, slot):
        xs_ref, b_ref, c_ref, dt_ref = src
        with_output = c_ref is not None
        rows = rows_of(c)
        dtc = dt_ref[rows, :]
        acum_all = _dot01_l(tri_b, dtc * a_row)
        yield
        shift = lax.rem(128 - d * SSD_H - grp * SSD_HPG, 128)
        acum = pltpu.roll(acum_all, shift, axis=1)
        dtg = pltpu.roll(dtc, shift, axis=1)
        a_last = acum[last:last + 1, :]
        w = jnp.exp2(a_last - acum) * dtg
        pack = jnp.where(lane < SSD_HPG, acum - jnp.log2(dtg), pltpu.roll(w, SSD_HPG, axis=1))
        bc = b_ref[rows, :]
        pre_ref[slot, d, 0] = acum
        pre_ref[slot, d, 1] = pack.T
        pre_ref[slot, d, 2] = bc.astype(F32).T
        dec_ref[slot, d] = jnp.broadcast_to(jnp.exp2(a_last), (8, 128))
        if with_output:
            pre_ref[slot, d, 3] = _dot_nt(c_ref[rows, :], bc)

    def update(src, c, d, tri, s_ref, finish, slot):
        xs_ref, b_ref, c_ref, dt_ref = src
        with_output = c_ref is not None
        rows = rows_of(c)
        acum_ref, pack_ref, bt_ref, cb_ref = (pre_ref.at[slot, d, j] for j in range(4))
        xs = xs_ref[rows, :]
        if with_output:
            ccf = c_ref[rows, :].astype(F32)
        ys = []
        for hp in range(npair):
            if hp:
                yield
            heads = (2 * hp, 2 * hp + 1)
            xf = xs[:, hp * 128:(hp + 1) * 128].astype(F32)
            xhalf = [jnp.where(lo_half, xf, 0.0).astype(BF16), jnp.where(lo_half, 0.0, xf).astype(BF16)]
            s_pair = s_ref[hp]
            ds = [_dot((bt_ref[...] * pack_ref[SSD_HPG + hh:SSD_HPG + hh + 1, :]).astype(BF16), xhalf[j])
                  for j, hh in enumerate(heads)]
            decs = [jnp.broadcast_to(dec_ref[slot, d, 0:1, hh:hh + 1], (SSD_N, 128)) for hh in heads]
            s_ref[hp] = s_pair * jnp.concatenate(decs, axis=0) + jnp.concatenate(ds, axis=0)
            if not with_output:
                continue
            lhs, clhs = [], []
            for hh in heads:
                acol = jnp.broadcast_to(acum_ref[:, hh:hh + 1], (L, L))
                m = cb_ref[...] * jnp.exp2(jnp.where(tri, acol - pack_ref[hh:hh + 1, :], -1e30))
                lhs.append(m.astype(BF16))
                clhs.append((ccf * jnp.exp2(acol)).astype(BF16))
            rhs = jnp.concatenate(xhalf + [s_pair.astype(BF16)], axis=0)
            ys.append(_dot(jnp.concatenate(lhs + clhs, axis=1), rhs))
        if not with_output:
            return
        yield
        y = jnp.concatenate(ys, axis=1)
        if not finish:
            y_acc[rows, :] = y
            return
        y = y + y_acc[rows, :] + dexp_ref[...] * xs.astype(F32)
        y = y * _silu(z_ref[rows, :].astype(F32))
        yn_ref[rows, :] = _rms(y, ng_ref[...]).astype(BF16)

    unr = 2 if n % 4 == 0 else 1

    def prepares(first, bank):
        out = []
        for u in range(unr):
            i = first + u
            out += [prepare(latent, jnp.minimum(i, n - 1), 0, tril_b, L - 1, bank * unr + u),
                    prepare(latent, jnp.maximum(n - 1 - i, 0), 1, triu_b, 0, bank * unr + u)]
        return out

    def body(finish, iu, carry):
        bank = lax.rem(iu, 2)
        ups = []
        for u in range(unr):
            i = iu * unr + u
            ups += [update(latent, i, 0, tril, sf_ref, finish, bank * unr + u),
                    update(latent, n - 1 - i, 1, triu, sb_ref, finish, bank * unr + u)]
        _lockstep(*(ups + prepares((iu + 1) * unr, 1 - bank)))
        return carry

    for i in range(nc):
        _lockstep(prepare(context, i, 0, tril_b, L - 1, 0), prepare(context, nc - 1 - i, 1, triu_b, 0, 0))
        _lockstep(update(context, i, 0, tril, sf_ref, False, 0),
                  update(context, nc - 1 - i, 1, triu, sb_ref, False, 0))
    _lockstep(*prepares(0, 0))
    half = n // (2 * unr)
    lax.fori_loop(0, half, functools.partial(body, False), 0)
    lax.fori_loop(half, 2 * half, functools.partial(body, True), 0)


def _ssd(xbc, dt, z, cxb, cdt, alog, dexp, ng):
    bsz, t, _ = xbc.shape
    tc = cxb.shape[1]
    assert (t // SSD_CHUNK) % 2 == 0 and tc % SSD_CHUNK == 0
    gw = SSD_HPG * SSD_P
    nb0 = SSD_INNER // SSD_N
    tok = lambda n, w, f: pl.BlockSpec((None, n, w), lambda b, g: (b, 0, f(g)))
    row128 = pl.BlockSpec((1, 128), lambda b, g: (0, 0))
    grow = pl.BlockSpec((1, gw), lambda b, g: (0, g))
    return pl.pallas_call(
        _ssd_kernel,
        grid=(bsz, SSD_G),
        in_specs=[tok(t, gw, lambda g: g), tok(t, SSD_N, lambda g: nb0 + g),
                  tok(t, SSD_N, lambda g: nb0 + SSD_G + g), tok(t, 128, lambda g: 0), tok(t, gw, lambda g: g),
                  row128, grow, grow,
                  tok(tc, gw, lambda g: g), tok(tc, SSD_N, lambda g: nb0 + g), tok(tc, 128, lambda g: 0)],
        out_specs=tok(t, gw, lambda g: g),
        out_shape=jax.ShapeDtypeStruct((bsz, t, SSD_INNER), BF16),
        scratch_shapes=[
            pltpu.VMEM((4, 2, 4, SSD_CHUNK, 128), F32), pltpu.VMEM((4, 2, 8, 128), F32),
            pltpu.VMEM((t, gw), F32),
            pltpu.VMEM((SSD_HPG // 2, 2 * SSD_N, 128), F32),
            pltpu.VMEM((SSD_HPG // 2, 2 * SSD_N, 128), F32)],
        compiler_params=_cparams(2),
        name="ssd",
    )(xbc, xbc, xbc, dt, z, alog, dexp, ng, cxb, cxb, cdt)


def _tile(t, pref):
    return pref if t % pref == 0 else t


def kernel(x, c, ctx, c_ctx, mod_w, mod_b, norm_g, ffn_w_in, ffn_w_out, ab_w_in, ab_gate_w, ab_gate_b,
           ab_gla_norm_g, ab_vnorm_g, ab_spatial_w, ab_spatial_b, ab_w_out, ssd_w_in, ssd_conv_w,
           ssd_conv_b, ssd_dt_bias, ssd_a_log, ssd_d, ssd_norm_g, ssd_w_out, final_norm_g):
    bsz, t, _ = x.shape
    tc = ctx.shape[1]
    assert mod_w.shape[0] == 2 and bsz <= 7 and tc % 128 == 0
    assert t % TOKEN_TILE == 0 and TOKEN_TILE % (8 * GRID_W) == 0 and (TOKEN_TILE * GRID_W) % (8 * t) == 0
    ctx_row = bsz

    cc = jnp.zeros((8, D), F32).at[:bsz].set(c).at[ctx_row].set(c_ctx)
    mods = _mods(cc, mod_w, mod_b)
    m0 = mods[0].reshape(8, 1, N_MOD * D)
    m1 = mods[1].reshape(8, 1, N_MOD * D)

    w = ab_w_in[0]
    o_k, o_v, o_lf, o_lb, o_q, o_r, o_u, o_g = 0, 256, 768, 784, 800, 1056, 1568, 2080
    w0 = jnp.concatenate([w[:, o_v:o_v + 512], w[:, o_r:o_r + 512], w[:, o_u:o_u + 512],
                          w[:, o_g:o_g + 512], w[:, o_k:o_k + 256], w[:, o_q:o_q + 256],
                          w[:, o_lf:o_lf + 32], jnp.zeros((D, 96), F32)], axis=1).astype(BF16)
    gw = jnp.zeros((128, 512), F32)
    gw = gw.at[0:GLA_LR, 0:256].set(ab_gate_w[0, 0]).at[GLA_LR:2 * GLA_LR, 256:512].set(ab_gate_w[0, 1])
    gw = gw.astype(BF16)
    gb = ab_gate_b[0].reshape(1, 512)
    gng = ab_gla_norm_g[0].reshape(1, 512)
    vng = ab_vnorm_g[0].reshape(1, 512)
    sw = ab_spatial_w[0].astype(BF16)
    sbt = ab_spatial_b[0].T
    wo0 = ab_w_out[0].astype(BF16)
    n1g = [norm_g[i, 0].reshape(1, D) for i in range(2)]
    n2g = [norm_g[i, 1].reshape(1, D) for i in range(2)]
    wi = ffn_w_in.astype(BF16)
    wo2 = ffn_w_out.astype(BF16)

    tmc = _tile(tc, 256)
    tmx = TOKEN_TILE
    nctx = bsz * tc
    tmf = _tile(nctx, TOKEN_TILE)
    ctxf = ctx.reshape(1, nctx, D)
    pcf, lacf = _proj0(ctxf, m0, ctx_row, n1g[0], w0, gw, gb, tmf)
    px, lax_ = _proj0(x, m0, None, n1g[0], w0, gw, gb, tmx)
    ac, ax = _gla(pcf.reshape(bsz, tc, P0_COLS), lacf.reshape(bsz, tc, 512), px, lax_, gng)
    ctx1 = _post0(ctxf, m0, ctx_row, ac.reshape(1, nctx, 512), pcf, vng, sw, sbt, wo0, n2g[0], wi, wo2, 0,
                  tmf, False).reshape(bsz, tc, D)
    x1 = _post0(x, m0, None, ax, px, vng, sw, sbt, wo0, n2g[0], wi, wo2, 0, tmx, True)
    x1 = x1.reshape(bsz, t, D)

    wall = ssd_w_in[0].astype(BF16)
    wz = wall[:, SSD_XBC + 2 * SSD_H:]
    cw = ssd_conv_w[0]
    cb = ssd_conv_b[0].reshape(1, SSD_XBC)
    dtb = jnp.concatenate([ssd_dt_bias[0].reshape(1, 2 * SSD_H), jnp.zeros((1, 64), F32)], axis=1)
    alog = jnp.concatenate([ssd_a_log[0].reshape(1, 2 * SSD_H), jnp.zeros((1, 64), F32)], axis=1)
    dexp = jnp.repeat(ssd_d[0], SSD_P).reshape(1, SSD_INNER)
    sng = ssd_norm_g[0].reshape(1, SSD_INNER)
    wo1 = ssd_w_out[0].astype(BF16)

    xb_c, dt_c = _proj1(ctx1, m1, ctx_row, n1g[1], wall, wz, cw, cb, dtb, tmc, True)
    xbc_x, z_x, dt_x = _proj1(x1, m1, None, n1g[1], wall, wz, cw, cb, dtb, tmx, False)
    yn = _ssd(xbc_x, dt_x, z_x, xb_c, dt_c, alog, dexp, sng)
    out = _post1(x1, m1, yn, wo1, n2g[1], wi, wo2, 1, final_norm_g.reshape(1, D), tmx)
    return out.reshape(bsz, t, D)
```

```python
import functools
import itertools

import jax
import jax.numpy as jnp
from jax import lax
from jax.experimental import pallas as pl
from jax.experimental.pallas import tpu as pltpu

F32 = jnp.float32
BF16 = jnp.bfloat16

D = 1024
N_MOD = 6
EPS = 1e-6
LOG2E = 1.4426950408889634
D_FF = 2816
GRID_W = 64

GLA_H, GLA_DK, GLA_DV, GLA_LR, GLA_CHUNK = 4, 64, 128, 16, 64
GLA_TAU_INV = 1.0 / 16.0
GMLP_W, GMLP_G, GMLP_CHUNK = 512, 4, 128
P0_COLS = 2560
P0_W = P0_COLS + 128

SSD_INNER, SSD_P, SSD_H, SSD_G, SSD_HPG, SSD_N, SSD_CHUNK, SSD_CONV = 2048, 64, 32, 4, 8, 128, 128, 5
SSD_GS = SSD_G * SSD_N
SSD_XBC = SSD_INNER + 2 * SSD_GS
HALO = 8
TOKEN_TILE = 512

VMEM_LIMIT = 56 * 1024 * 1024


def _cparams(n_axes):
    return pltpu.CompilerParams(dimension_semantics=("arbitrary",) * n_axes,
                                vmem_limit_bytes=VMEM_LIMIT)


def _dot(a, b):
    return jnp.dot(a, b, preferred_element_type=F32)


def _dot_nt(a, b):
    return lax.dot_general(a, b, (((1,), (1,)), ((), ())), preferred_element_type=F32)


def _dot_tn(a, b):
    return lax.dot_general(a, b, (((0,), (0,)), ((), ())), preferred_element_type=F32)


def _split3(x):
    hi = x.astype(BF16)
    r1 = x - hi.astype(F32)
    mid = r1.astype(BF16)
    lo = (r1 - mid.astype(F32)).astype(BF16)
    return hi, mid, lo


def _dot01_l(t01, x):
    hi, mid, lo = _split3(x)
    return (_dot(t01, lo) + _dot(t01, mid)) + _dot(t01, hi)


def _lockstep(*stages):
    for _ in itertools.zip_longest(*stages):
        pass


def _silu(x):
    return x * jax.nn.sigmoid(x)


def _softplus(x):
    return jnp.maximum(x, 0.0) + jnp.log1p(jnp.exp(-jnp.abs(x)))


def _log_sigmoid(x):
    return jnp.minimum(x, 0.0) - jnp.log1p(jnp.exp(-jnp.abs(x)))


def _rms(x, g):
    return x * lax.rsqrt(jnp.mean(x * x, axis=-1, keepdims=True) + EPS) * g


def _norm_mod(x, g, shift, scale):
    return _rms(x, g) * (1.0 + scale) + shift


def _mods_kernel(cc_ref, w_ref, b_ref, o_ref):
    s = _silu(cc_ref[...])
    o_ref[...] = jnp.dot(s, w_ref[...], precision=lax.Precision.HIGHEST,
                         preferred_element_type=F32) + b_ref[...]


def _mods(cc, mod_w, mod_b):
    depth = mod_w.shape[0]
    wb = N_MOD * D // 2
    return pl.pallas_call(
        _mods_kernel,
        grid=(depth, 2),
        in_specs=[pl.BlockSpec((8, D), lambda i, j: (0, 0)),
                  pl.BlockSpec((None, D, wb), lambda i, j: (i, 0, j)),
                  pl.BlockSpec((None, 1, wb), lambda i, j: (i, 0, j))],
        out_specs=pl.BlockSpec((None, 8, wb), lambda i, j: (i, 0, j)),
        out_shape=jax.ShapeDtypeStruct((depth, 8, N_MOD * D), F32),
        compiler_params=_cparams(2),
        name="mods",
    )(cc, mod_w, mod_b.reshape(depth, 1, N_MOD * D))


def _mod_spec(mod_row):
    if mod_row is None:
        return pl.BlockSpec((None, 1, N_MOD * D), lambda b, i: (b, 0, 0))
    return pl.BlockSpec((None, 1, N_MOD * D), lambda b, i: (mod_row, 0, 0))


def _const_spec(shape, index=None):
    index = (0,) * len(shape) if index is None else index
    return pl.BlockSpec(shape, lambda *_: index, pipeline_mode=pl.Buffered(1))


def _ffn_specs(layer):
    return [_const_spec((None, D, 2 * D_FF), (layer, 0, 0)), _const_spec((None, D_FF, D), (layer, 0, 0))]


def _proj0_kernel(x_ref, mod_ref, g_ref, w_ref, gw_ref, gb_ref, p_ref, la_ref):
    h = _norm_mod(x_ref[...], g_ref[...], mod_ref[:, 0:D], mod_ref[:, D:2 * D]).astype(BF16)
    for c0 in range(0, 2048, 512):
        p_ref[:, c0:c0 + 512] = _dot(h, w_ref[:, c0:c0 + 512]).astype(BF16)
    p_ref[:, 2048:2304] = _dot(h, w_ref[:, 2048:2304]).astype(BF16)
    p_ref[:, 2304:2560] = (_dot(h, w_ref[:, 2304:2560]) * (GLA_DK ** -0.5)).astype(BF16)
    lr = _dot(h, w_ref[:, P0_COLS:P0_W])
    z = _dot(lr.astype(BF16), gw_ref[...]) + gb_ref[...]
    la_ref[...] = _log_sigmoid(z) * GLA_TAU_INV


def _proj0(x, mods, mod_row, g, w, gw, gb, tm):
    bsz, t, _ = x.shape
    return pl.pallas_call(
        _proj0_kernel,
        grid=(bsz, t // tm),
        in_specs=[pl.BlockSpec((None, tm, D), lambda b, i: (b, i, 0)),
                  _mod_spec(mod_row),
                  _const_spec((1, D)),
                  _const_spec((D, P0_W)),
                  _const_spec((128, 512)),
                  _const_spec((1, 512))],
        out_specs=[pl.BlockSpec((None, tm, P0_COLS), lambda b, i: (b, i, 0)),
                   pl.BlockSpec((None, tm, 512), lambda b, i: (b, i, 0))],
        out_shape=[jax.ShapeDtypeStruct((bsz, t, P0_COLS), BF16),
                   jax.ShapeDtypeStruct((bsz, t, 512), F32)],
        compiler_params=_cparams(2),
        name="proj0",
    )(x, mods, g, w, gw, gb)


def _gla_kernel(*refs):
    ctx_in, lat_in, gng_ref = refs[0:6], refs[6:12], refs[12]
    ctx_a, lat_a, ctx_acc, lat_acc, sf_ref, sb_ref = refs[13:19]
    L = GLA_CHUNK
    row = lax.broadcasted_iota(jnp.int32, (L, L), 0)
    col = lax.broadcasted_iota(jnp.int32, (L, L), 1)
    tril, triu = col <= row, col >= row
    tril_b, triu_b = tril.astype(BF16), triu.astype(BF16)
    bd_mask = (lax.broadcasted_iota(jnp.int32, (256, 128), 0) // GLA_DV
               == lax.broadcasted_iota(jnp.int32, (256, 128), 1) // GLA_DK)
    lane = lax.broadcasted_iota(jnp.int32, (L, 128), 1)
    head_masks = (lane < GLA_DK, lane >= GLA_DK)

    sf_ref[...] = jnp.zeros_like(sf_ref)
    sb_ref[...] = jnp.zeros_like(sb_ref)

    def scan(k_ref, q_ref, v_ref, r_ref, laf_ref, lab_ref, a_ref, o_acc):
        n = k_ref.shape[0] // L

        def one(rows, la_ref, tri, tri_b, last, s_ref, finish):
            k = k_ref[rows, :].astype(F32)
            q = q_ref[rows, :].astype(F32)
            v = v_ref[rows, :]
            b = _dot01_l(tri_b, la_ref[rows, :])
            yield
            b_last = b[last:last + 1, :]
            qd = q * jnp.exp(b)
            ki = (k * jnp.exp(-b)).astype(BF16)
            kd = (k * jnp.exp(b_last - b)).astype(BF16)
            sc = [_dot_nt(jnp.where(head_masks[h], qd, 0.0).astype(BF16), ki) for h in range(2)]
            s_prev = s_ref[...]
            o = _dot_nt(qd.astype(BF16), s_prev.astype(BF16))
            s_ref[...] = s_prev * jnp.exp(b_last) + jnp.where(bd_mask, _dot_tn(v, kd), 0.0)
            yield
            intra = [_dot(jnp.where(tri, sc[h], 0.0).astype(BF16), v[:, h * GLA_DV:(h + 1) * GLA_DV])
                     for h in range(2)]
            o = o + jnp.concatenate(intra, axis=1)
            if not finish:
                o_acc[rows, :] = o
                return
            o = o + o_acc[rows, :]
            r = r_ref[rows, :].astype(F32)
            outs = []
            for h in range(2):
                sl = slice(h * GLA_DV, (h + 1) * GLA_DV)
                outs.append(_rms(o[:, sl], gng_ref[:, sl]) * _silu(r[:, sl]))
            a_ref[rows, :] = jnp.concatenate(outs, axis=1).astype(BF16)

        unroll = 8 if n % 16 == 0 else (4 if n % 8 == 0 else 2)

        def body(finish, iu, carry):
            stages = []
            for u in range(unroll):
                i = iu * unroll + u
                stages.append(one(pl.ds(pl.multiple_of(i * L, L), L), laf_ref, tril, tril_b, L - 1, sf_ref,
                                  finish))
                stages.append(one(pl.ds(pl.multiple_of((n - 1 - i) * L, L), L), lab_ref, triu, triu_b, 0,
                                  sb_ref, finish))
            _lockstep(*stages)
            return carry

        half = n // (2 * unroll)
        lax.fori_loop(0, half, functools.partial(body, False), 0)
        lax.fori_loop(half, 2 * half, functools.partial(body, True), 0)

    scan(*ctx_in, ctx_a, ctx_acc)
    scan(*lat_in, lat_a, lat_acc)


def _gla(pc, lac, p, la, gng):
    bsz, t, _ = p.shape
    tc = pc.shape[1]
    assert (t // GLA_CHUNK) % 4 == 0 and (tc // GLA_CHUNK) % 4 == 0
    npair = GLA_H // 2
    kb, qb = 2048 // 128, 2304 // 128

    def operands(n):
        tok = lambda w, f: pl.BlockSpec((None, n, w), lambda b, h: (b, 0, f(h)))
        return [tok(128, lambda h: kb + h), tok(128, lambda h: qb + h),
                tok(256, lambda h: h), tok(256, lambda h: npair + h),
                tok(128, lambda h: h), tok(128, lambda h: npair + h)]

    out = lambda n: pl.BlockSpec((None, n, 256), lambda b, h: (b, 0, h))
    return pl.pallas_call(
        _gla_kernel,
        grid=(bsz, npair),
        in_specs=operands(tc) + operands(t) + [pl.BlockSpec((1, 256), lambda b, h: (0, h))],
        out_specs=[out(tc), out(t)],
        out_shape=[jax.ShapeDtypeStruct((bsz, tc, 512), BF16), jax.ShapeDtypeStruct((bsz, t, 512), BF16)],
        scratch_shapes=[pltpu.VMEM((tc, 256), F32), pltpu.VMEM((t, 256), F32),
                        pltpu.VMEM((256, 128), F32), pltpu.VMEM((256, 128), F32)],
        compiler_params=_cparams(2),
        name="gla",
    )(pc, pc, pc, pc, lac, lac, p, p, p, p, la, la, gng)


def _ffn_tail(x1, rows, mod_ref, n2g_ref, wi_ref, wo2_ref, act_ref):
    h2 = _norm_mod(x1, n2g_ref[...], mod_ref[:, 3 * D:4 * D], mod_ref[:, 4 * D:5 * D]).astype(BF16)
    yield
    for c0 in range(0, D_FF, 512):
        c1 = min(c0 + 512, D_FF)
        gate = _dot(h2, wi_ref[:, c0:c1])
        up = _dot(h2, wi_ref[:, D_FF + c0:D_FF + c1])
        act_ref[rows, c0:c1] = (_silu(gate) * up).astype(BF16)
    yield
    return x1 + mod_ref[:, 5 * D:6 * D] * _dot(act_ref[rows, :], wo2_ref[...])


def _staggered_halves(tm, part):
    sub = tm // 2 if tm % (2 * GMLP_CHUNK) == 0 else tm

    def delayed(j):
        for _ in range(j):
            yield
        yield from part(slice(j * sub, (j + 1) * sub))

    _lockstep(*[delayed(j) for j in range(tm // sub)])


def _tile_dst(o_ref, out_ref, k):
    return o_ref if k == 1 else out_ref


def _store_transposed(o_ref, out_ref, r, k):
    for j in range(k):
        o_ref[:, j, :] = out_ref[j * r:(j + 1) * r, :]


def _post0_kernel(x_ref, mod_ref, a_ref, u_ref, gv_ref, vng_ref, sw_ref, sbt_ref, wo_ref,
                  n2g_ref, wi_ref, wo2_ref, o_ref, mix_ref, act_ref, out_ref, *, r, k):
    def part(rows):
        uu = jax.nn.gelu(u_ref[rows, :].astype(F32), approximate=True)
        vv = jax.nn.gelu(gv_ref[rows, :].astype(F32), approximate=True)
        mu = jnp.mean(vv, axis=-1, keepdims=True)
        vc = vv - mu
        vn = (vc * lax.rsqrt(jnp.mean(vc * vc, axis=-1, keepdims=True) + EPS) * vng_ref[...]).astype(BF16)
        mix_ref[rows, 0:512] = a_ref[rows, :]
        for ci in range((rows.stop - rows.start) // GMLP_CHUNK):
            rs = slice(ci * GMLP_CHUNK, (ci + 1) * GMLP_CHUNK)
            ms = slice(rows.start + rs.start, rows.start + rs.stop)
            for gi in range(GMLP_G):
                cs = slice(gi * 128, (gi + 1) * 128)
                s = _dot(sw_ref[gi], vn[rs, cs]) + sbt_ref[:, gi:gi + 1]
                mix_ref[ms, 512 + gi * 128:512 + (gi + 1) * 128] = (uu[rs, cs] * s).astype(BF16)
        x1 = x_ref[rows, :] + mod_ref[:, 2 * D:3 * D] * _dot(mix_ref[rows, :], wo_ref[...])
        yield
        dst[rows, :] = yield from _ffn_tail(x1, rows, mod_ref, n2g_ref, wi_ref, wo2_ref, act_ref)

    dst = _tile_dst(o_ref, out_ref, k)
    _staggered_halves(r * k, part)
    if k > 1:
        _store_transposed(o_ref, out_ref, r, k)


def _post1_kernel(x_ref, mod_ref, yn_ref, wo_ref, n2g_ref, wi_ref, wo2_ref, fg_ref,
                  o_ref, act_ref, out_ref, *, r, k):
    def part(rows):
        x1 = x_ref[rows, :] + mod_ref[:, 2 * D:3 * D] * _dot(yn_ref[rows, :], wo_ref[...])
        yield
        x2 = yield from _ffn_tail(x1, rows, mod_ref, n2g_ref, wi_ref, wo2_ref, act_ref)
        dst[rows, :] = _rms(x2, fg_ref[...])

    dst = _tile_dst(o_ref, out_ref, k)
    _staggered_halves(r * k, part)
    if k > 1:
        _store_transposed(o_ref, out_ref, r, k)


def _post0(x, mods, mod_row, a, p, vng, sw, sbt, wo, n2g, wi, wo2, layer, tm, transpose_grid):
    bsz, t, _ = x.shape
    tok = lambda w, j: pl.BlockSpec((None, tm, w), lambda b, i: (b, i, j))
    if transpose_grid:
        k = tm // GRID_W
        out_spec = pl.BlockSpec((None, GRID_W, k, D), lambda b, i: (b, 0, i, 0))
        out_shape = jax.ShapeDtypeStruct((bsz, GRID_W, t // GRID_W, D), F32)
        kern = functools.partial(_post0_kernel, r=GRID_W, k=k)
    else:
        out_spec, out_shape = tok(D, 0), jax.ShapeDtypeStruct((bsz, t, D), F32)
        kern = functools.partial(_post0_kernel, r=tm, k=1)
    return pl.pallas_call(
        kern,
        grid=(bsz, t // tm),
        in_specs=[tok(D, 0), _mod_spec(mod_row), tok(512, 0), tok(512, 2), tok(512, 3),
                  _const_spec((1, 512)), _const_spec((GMLP_G, 128, 128)), _const_spec((128, GMLP_G)),
                  _const_spec((D, D)), _const_spec((1, D))] + _ffn_specs(layer),
        out_specs=out_spec,
        out_shape=out_shape,
        scratch_shapes=[pltpu.VMEM((tm, D), BF16), pltpu.VMEM((tm, D_FF), BF16), pltpu.VMEM((tm, D), F32)],
        compiler_params=_cparams(2),
        name="post0",
    )(x, mods, a, p, p, vng, sw, sbt, wo, n2g, wi, wo2)


def _post1(x, mods, yn, wo, n2g, wi, wo2, layer, fg, tm):
    bsz, t, _ = x.shape
    r = t // GRID_W
    k = tm // r
    tok = lambda w: pl.BlockSpec((None, tm, w), lambda b, i: (b, i, 0))
    return pl.pallas_call(
        functools.partial(_post1_kernel, r=r, k=k),
        grid=(bsz, t // tm),
        in_specs=[tok(D), _mod_spec(None), tok(SSD_INNER),
                  _const_spec((SSD_INNER, D)), _const_spec((1, D))] + _ffn_specs(layer) + [_const_spec((1, D))],
        out_specs=pl.BlockSpec((None, r, k, D), lambda b, i: (b, 0, i, 0)),
        out_shape=jax.ShapeDtypeStruct((bsz, r, GRID_W, D), F32),
        scratch_shapes=[pltpu.VMEM((tm, D_FF), BF16), pltpu.VMEM((tm, D), F32)],
        compiler_params=_cparams(2),
        name="post1",
    )(x, mods, yn, wo, n2g, wi, wo2, fg)


def _proj1_kernel(*refs, state_only):
    if state_only:
        (xm_ref, xp_ref, xn_ref, mod_ref, g_ref, wxbc_ref, wdt_ref, cw_ref, cb_ref, dtb_ref,
         xbc_ref, dt_ref, h_ext, p_ext) = refs
    else:
        (xm_ref, xp_ref, xn_ref, mod_ref, g_ref, wxbc_ref, wz_ref, wdt_ref, cw_ref, cb_ref, dtb_ref,
         xbc_ref, z_ref, dt_ref, h_ext, p_ext) = refs
    tm = xm_ref.shape[0]
    i = pl.program_id(1)
    last = pl.num_programs(1) - 1
    g = g_ref[...]
    shift, scale = mod_ref[:, 0:D], mod_ref[:, D:2 * D]
    h_ext[0:HALO, :] = jnp.where(i > 0, _norm_mod(xp_ref[...], g, shift, scale), 0.0)
    h_ext[HALO:HALO + tm, :] = _norm_mod(xm_ref[...], g, shift, scale)
    h_ext[HALO + tm:2 * HALO + tm, :] = jnp.where(i < last, _norm_mod(xn_ref[...], g, shift, scale), 0.0)
    hb = h_ext[...].astype(BF16)
    pad = (SSD_CONV - 1) // 2
    nslab = p_ext.shape[0]
    cwid = nslab * 128
    for c0 in range(0, xbc_ref.shape[1], cwid):
        res = _dot(hb, wxbc_ref[:, c0:c0 + cwid])
        for s in range(nslab):
            p_ext[s] = res[:, s * 128:(s + 1) * 128]
        for s in range(nslab):
            cs = slice(c0 + s * 128, c0 + (s + 1) * 128)
            acc = cb_ref[:, cs] + cw_ref[0:1, cs] * p_ext[s, HALO - pad:HALO - pad + tm, :]
            for j in range(1, SSD_CONV):
                acc = acc + cw_ref[j:j + 1, cs] * p_ext[s, HALO - pad + j:HALO - pad + j + tm, :]
            xbc_ref[:, cs] = _silu(acc).astype(BF16)
    hm = h_ext[HALO:HALO + tm, :].astype(BF16)
    if not state_only:
        for c0 in range(0, SSD_INNER, 512):
            z_ref[:, c0:c0 + 512] = _dot(hm, wz_ref[:, c0:c0 + 512]).astype(BF16)
    dt_ref[...] = _softplus(_dot(hm, wdt_ref[...]) + dtb_ref[...])


def _proj1(x, mods, mod_row, g, wall, wz, cw, cb, dtb, tm, state_only):
    bsz, t, _ = x.shape
    hpt = tm // HALO
    tok = lambda w: pl.BlockSpec((None, tm, w), lambda b, i: (b, i, 0))
    shp = lambda w, dt: jax.ShapeDtypeStruct((bsz, t, w), dt)
    ncol = SSD_INNER + SSD_GS if state_only else SSD_XBC
    w_specs = [_const_spec((D, ncol))] + ([] if state_only else [_const_spec((D, SSD_INNER))])
    w_specs += [_const_spec((D, 128), (0, SSD_XBC // 128))]
    w_args = (wall,) + (() if state_only else (wz,)) + (wall,)
    return pl.pallas_call(
        functools.partial(_proj1_kernel, state_only=state_only),
        grid=(bsz, t // tm),
        in_specs=[tok(D),
                  pl.BlockSpec((None, HALO, D), lambda b, i: (b, jnp.maximum(i * hpt - 1, 0), 0)),
                  pl.BlockSpec((None, HALO, D), lambda b, i: (b, jnp.minimum((i + 1) * hpt, t // HALO - 1), 0)),
                  _mod_spec(mod_row), _const_spec((1, D))] + w_specs +
                 [_const_spec((SSD_CONV, ncol)), _const_spec((1, ncol)), _const_spec((1, 128))],
        out_specs=[tok(ncol)] + ([] if state_only else [tok(SSD_INNER)]) + [tok(128)],
        out_shape=[shp(ncol, BF16)] + ([] if state_only else [shp(SSD_INNER, BF16)]) + [shp(128, F32)],
        scratch_shapes=[pltpu.VMEM((tm + 2 * HALO, D), F32), pltpu.VMEM((4, tm + 2 * HALO, 128), F32)],
        compiler_params=_cparams(2),
        name="proj1_state" if state_only else "proj1",
    )(x, x, x, mods, g, *w_args, cw, cb, dtb)


def _ssd_kernel(xs_ref, b_ref, c_ref, dt_ref, z_ref, alog_ref, dexp_ref, ng_ref, cxs_ref, cb_ref_, cdt_ref,
                yn_ref, pre_ref, dec_ref, y_acc, sf_ref, sb_ref):
    L = SSD_CHUNK
    n = xs_ref.shape[0] // L
    nc = cxs_ref.shape[0] // L
    npair = SSD_HPG // 2
    grp = pl.program_id(1)
    row = lax.broadcasted_iota(jnp.int32, (L, L), 0)
    col = lax.broadcasted_iota(jnp.int32, (L, L), 1)
    tril, triu = col <= row, col >= row
    tril_b, triu_b = tril.astype(BF16), triu.astype(BF16)
    lane = lax.broadcasted_iota(jnp.int32, (L, 128), 1)
    lo_half = lane < SSD_P
    a_row = -jnp.exp(alog_ref[...]) * LOG2E

    sf_ref[...] = jnp.zeros_like(sf_ref)
    sb_ref[...] = jnp.zeros_like(sb_ref)
    latent = (xs_ref, b_ref, c_ref, dt_ref)
    context = (cxs_ref, cb_ref_, None, cdt_ref)

    def rows_of(c):
        return pl.ds(c * L if isinstance(c, int) else pl.multiple_of(c * L, L), L)

    def prepare(src, c, d, tri_b, last, slot):
        xs_ref, b_ref, c_ref, dt_ref = src
        with_output = c_ref is not None
        rows = rows_of(c)
        dtc = dt_ref[rows, :]
        acum_all = _dot01_l(tri_b, dtc * a_row)
        yield
        shift = lax.rem(128 - d * SSD_H - grp * SSD_HPG, 128)
        acum = pltpu.roll(acum_all, shift, axis=1)
        dtg = pltpu.roll(dtc, shift, axis=1)
        a_last = acum[last:last + 1, :]
        w = jnp.exp2(a_last - acum) * dtg
        pack = jnp.where(lane < SSD_HPG, acum - jnp.log2(dtg), pltpu.roll(w, SSD_HPG, axis=1))
        yield
        bc = b_ref[rows, :]
        pre_ref[slot, d, 0] = acum
        pre_ref[slot, d, 1] = pack.T
        pre_ref[slot, d, 2] = bc.astype(F32).T
        dec_ref[slot, d] = jnp.broadcast_to(jnp.exp2(a_last), (8, 128))
        if with_output:
            pre_ref[slot, d, 3] = _dot_nt(c_ref[rows, :], bc)

    def update(src, c, d, tri, s_ref, finish, slot):
        xs_ref, b_ref, c_ref, dt_ref = src
        with_output = c_ref is not None
        rows = rows_of(c)
        acum_ref, pack_ref, bt_ref, cb_ref = (pre_ref.at[slot, d, j] for j in range(4))
        xs = xs_ref[rows, :]
        if with_output:
            ccf = c_ref[rows, :].astype(F32)
        ys = []
        for hp in range(npair):
            if hp == npair // 2:
                yield
            heads = (2 * hp, 2 * hp + 1)
            xf = xs[:, hp * 128:(hp + 1) * 128].astype(F32)
            xhalf = [jnp.where(lo_half, xf, 0.0).astype(BF16), jnp.where(lo_half, 0.0, xf).astype(BF16)]
            s_pair = s_ref[hp]
            ds = [_dot((bt_ref[...] * pack_ref[SSD_HPG + hh:SSD_HPG + hh + 1, :]).astype(BF16), xhalf[j])
                  for j, hh in enumerate(heads)]
            decs = [jnp.broadcast_to(dec_ref[slot, d, 0:1, hh:hh + 1], (SSD_N, 128)) for hh in heads]
            s_ref[hp] = s_pair * jnp.concatenate(decs, axis=0) + jnp.concatenate(ds, axis=0)
            if not with_output:
                continue
            lhs, clhs = [], []
            for hh in heads:
                acol = jnp.broadcast_to(acum_ref[:, hh:hh + 1], (L, L))
                m = cb_ref[...] * jnp.exp2(jnp.where(tri, acol - pack_ref[hh:hh + 1, :], -1e30))
                lhs.append(m.astype(BF16))
                clhs.append((ccf * jnp.exp2(acol)).astype(BF16))
            rhs = jnp.concatenate(xhalf + [s_pair.astype(BF16)], axis=0)
            ys.append(_dot(jnp.concatenate(lhs + clhs, axis=1), rhs))
        if not with_output:
            return
        yield
        y = jnp.concatenate(ys, axis=1)
        if not finish:
            y_acc[rows, :] = y
            return
        y = y + y_acc[rows, :] + dexp_ref[...] * xs.astype(F32)
        y = y * _silu(z_ref[rows, :].astype(F32))
        yn_ref[rows, :] = _rms(y, ng_ref[...]).astype(BF16)

    unr = 2 if n % 4 == 0 else 1

    def prepares(first, bank):
        out = []
        for u in range(unr):
            i = first + u
            out += [prepare(latent, jnp.minimum(i, n - 1), 0, tril_b, L - 1, bank * unr + u),
                    prepare(latent, jnp.maximum(n - 1 - i, 0), 1, triu_b, 0, bank * unr + u)]
        return out

    def body(finish, iu, carry):
        bank = lax.rem(iu, 2)
        ups = []
        for u in range(unr):
            i = iu * unr + u
            ups += [update(latent, i, 0, tril, sf_ref, finish, bank * unr + u),
                    update(latent, n - 1 - i, 1, triu, sb_ref, finish, bank * unr + u)]
        _lockstep(*(ups + prepares((iu + 1) * unr, 1 - bank)))
        return carry

    for i in range(nc):
        _lockstep(prepare(context, i, 0, tril_b, L - 1, 0), prepare(context, nc - 1 - i, 1, triu_b, 0, 0))
        _lockstep(update(context, i, 0, tril, sf_ref, False, 0),
                  update(context, nc - 1 - i, 1, triu, sb_ref, False, 0))
    _lockstep(*prepares(0, 0))
    half = n // (2 * unr)
    lax.fori_loop(0, half, functools.partial(body, False), 0)
    lax.fori_loop(half, 2 * half, functools.partial(body, True), 0)


def _ssd(xbc, dt, z, cxb, cdt, alog, dexp, ng):
    bsz, t, _ = xbc.shape
    tc = cxb.shape[1]
    assert (t // SSD_CHUNK) % 2 == 0 and tc % SSD_CHUNK == 0
    gw = SSD_HPG * SSD_P
    nb0 = SSD_INNER // SSD_N
    tok = lambda n, w, f: pl.BlockSpec((None, n, w), lambda b, g: (b, 0, f(g)))
    row128 = pl.BlockSpec((1, 128), lambda b, g: (0, 0))
    grow = pl.BlockSpec((1, gw), lambda b, g: (0, g))
    return pl.pallas_call(
        _ssd_kernel,
        grid=(bsz, SSD_G),
        in_specs=[tok(t, gw, lambda g: g), tok(t, SSD_N, lambda g: nb0 + g),
                  tok(t, SSD_N, lambda g: nb0 + SSD_G + g), tok(t, 128, lambda g: 0), tok(t, gw, lambda g: g),
                  row128, grow, grow,
                  tok(tc, gw, lambda g: g), tok(tc, SSD_N, lambda g: nb0 + g), tok(tc, 128, lambda g: 0)],
        out_specs=tok(t, gw, lambda g: g),
        out_shape=jax.ShapeDtypeStruct((bsz, t, SSD_INNER), BF16),
        scratch_shapes=[
            pltpu.VMEM((4, 2, 4, SSD_CHUNK, 128), F32), pltpu.VMEM((4, 2, 8, 128), F32),
            pltpu.VMEM((t, gw), F32),
            pltpu.VMEM((SSD_HPG // 2, 2 * SSD_N, 128), F32),
            pltpu.VMEM((SSD_HPG // 2, 2 * SSD_N, 128), F32)],
        compiler_params=_cparams(2),
        name="ssd",
    )(xbc, xbc, xbc, dt, z, alog, dexp, ng, cxb, cxb, cdt)


def _tile(t, pref):
    return pref if t % pref == 0 else t


def kernel(x, c, ctx, c_ctx, mod_w, mod_b, norm_g, ffn_w_in, ffn_w_out, ab_w_in, ab_gate_w, ab_gate_b,
           ab_gla_norm_g, ab_vnorm_g, ab_spatial_w, ab_spatial_b, ab_w_out, ssd_w_in, ssd_conv_w,
           ssd_conv_b, ssd_dt_bias, ssd_a_log, ssd_d, ssd_norm_g, ssd_w_out, final_norm_g):
    bsz, t, _ = x.shape
    tc = ctx.shape[1]
    assert mod_w.shape[0] == 2 and bsz <= 7 and tc % 128 == 0
    assert t % TOKEN_TILE == 0 and TOKEN_TILE % (8 * GRID_W) == 0 and (TOKEN_TILE * GRID_W) % (8 * t) == 0
    ctx_row = bsz

    cc = jnp.zeros((8, D), F32).at[:bsz].set(c).at[ctx_row].set(c_ctx)
    mods = _mods(cc, mod_w, mod_b)
    m0 = mods[0].reshape(8, 1, N_MOD * D)
    m1 = mods[1].reshape(8, 1, N_MOD * D)

    w = ab_w_in[0]
    o_k, o_v, o_lf, o_lb, o_q, o_r, o_u, o_g = 0, 256, 768, 784, 800, 1056, 1568, 2080
    w0 = jnp.concatenate([w[:, o_v:o_v + 512], w[:, o_r:o_r + 512], w[:, o_u:o_u + 512],
                          w[:, o_g:o_g + 512], w[:, o_k:o_k + 256], w[:, o_q:o_q + 256],
                          w[:, o_lf:o_lf + 32], jnp.zeros((D, 96), F32)], axis=1).astype(BF16)
    gw = jnp.zeros((128, 512), F32)
    gw = gw.at[0:GLA_LR, 0:256].set(ab_gate_w[0, 0]).at[GLA_LR:2 * GLA_LR, 256:512].set(ab_gate_w[0, 1])
    gw = gw.astype(BF16)
    gb = ab_gate_b[0].reshape(1, 512)
    gng = ab_gla_norm_g[0].reshape(1, 512)
    vng = ab_vnorm_g[0].reshape(1, 512)
    sw = ab_spatial_w[0].astype(BF16)
    sbt = ab_spatial_b[0].T
    wo0 = ab_w_out[0].astype(BF16)
    n1g = [norm_g[i, 0].reshape(1, D) for i in range(2)]
    n2g = [norm_g[i, 1].reshape(1, D) for i in range(2)]
    wi = ffn_w_in.astype(BF16)
    wo2 = ffn_w_out.astype(BF16)

    tmc = _tile(tc, 256)
    tmx = TOKEN_TILE
    nctx = bsz * tc
    tmf = _tile(nctx, TOKEN_TILE)
    ctxf = ctx.reshape(1, nctx, D)
    pcf, lacf = _proj0(ctxf, m0, ctx_row, n1g[0], w0, gw, gb, tmf)
    px, lax_ = _proj0(x, m0, None, n1g[0], w0, gw, gb, tmx)
    ac, ax = _gla(pcf.reshape(bsz, tc, P0_COLS), lacf.reshape(bsz, tc, 512), px, lax_, gng)
    ctx1 = _post0(ctxf, m0, ctx_row, ac.reshape(1, nctx, 512), pcf, vng, sw, sbt, wo0, n2g[0], wi, wo2, 0,
                  tmf, False).reshape(bsz, tc, D)
    x1 = _post0(x, m0, None, ax, px, vng, sw, sbt, wo0, n2g[0], wi, wo2, 0, tmx, True)
    x1 = x1.reshape(bsz, t, D)

    wall = ssd_w_in[0].astype(BF16)
    wz = wall[:, SSD_XBC + 2 * SSD_H:]
    cw = ssd_conv_w[0]
    cb = ssd_conv_b[0].reshape(1, SSD_XBC)
    dtb = jnp.concatenate([ssd_dt_bias[0].reshape(1, 2 * SSD_H), jnp.zeros((1, 64), F32)], axis=1)
    alog = jnp.concatenate([ssd_a_log[0].reshape(1, 2 * SSD_H), jnp.zeros((1, 64), F32)], axis=1)
    dexp = jnp.repeat(ssd_d[0], SSD_P).reshape(1, SSD_INNER)
    sng = ssd_norm_g[0].reshape(1, SSD_INNER)
    wo1 = ssd_w_out[0].astype(BF16)

    xb_c, dt_c = _proj1(ctx1, m1, ctx_row, n1g[1], wall, wz, cw, cb, dtb, tmc, True)
    xbc_x, z_x, dt_x = _proj1(x1, m1, None, n1g[1], wall, wz, cw, cb, dtb, tmx, False)
    yn = _ssd(xbc_x, dt_x, z_x, xb_c, dt_c, alog, dexp, sng)
    out = _post1(x1, m1, yn, wo1, n2g[1], wi, wo2, 1, final_norm_g.reshape(1, D), tmx)
    return out.reshape(bsz, t, D)
```

```python
import functools
import itertools

import jax
import jax.numpy as jnp
from jax import lax
from jax.experimental import pallas as pl
from jax.experimental.pallas import tpu as pltpu

F32 = jnp.float32
BF16 = jnp.bfloat16

D = 1024
N_MOD = 6
EPS = 1e-6
LOG2E = 1.4426950408889634
D_FF = 2816
GRID_W = 64

GLA_H, GLA_DK, GLA_DV, GLA_LR, GLA_CHUNK = 4, 64, 128, 16, 64
GLA_TAU_INV = 1.0 / 16.0
GMLP_W, GMLP_G, GMLP_CHUNK = 512, 4, 128
P0_COLS = 2560
P0_W = P0_COLS + 128

SSD_INNER, SSD_P, SSD_H, SSD_G, SSD_HPG, SSD_N, SSD_CHUNK, SSD_CONV = 2048, 64, 32, 4, 8, 128, 128, 5
SSD_GS = SSD_G * SSD_N
SSD_XBC = SSD_INNER + 2 * SSD_GS
HALO = 8
TOKEN_TILE = 512

VMEM_LIMIT = 56 * 1024 * 1024


def _cparams(n_axes):
    return pltpu.CompilerParams(dimension_semantics=("arbitrary",) * n_axes,
                                vmem_limit_bytes=VMEM_LIMIT)


def _dot(a, b):
    return jnp.dot(a, b, preferred_element_type=F32)


def _dot_nt(a, b):
    return lax.dot_general(a, b, (((1,), (1,)), ((), ())), preferred_element_type=F32)


def _dot_tn(a, b):
    return lax.dot_general(a, b, (((0,), (0,)), ((), ())), preferred_element_type=F32)


def _split3(x):
    hi = x.astype(BF16)
    r1 = x - hi.astype(F32)
    mid = r1.astype(BF16)
    lo = (r1 - mid.astype(F32)).astype(BF16)
    return hi, mid, lo


def _dot01_l(t01, x):
    hi, mid, lo = _split3(x)
    return (_dot(t01, lo) + _dot(t01, mid)) + _dot(t01, hi)


def _lockstep(*stages):
    for _ in itertools.zip_longest(*stages):
        pass


def _silu(x):
    h = 0.5 * x
    return h * jnp.tanh(h) + h


def _softplus(x):
    return jnp.maximum(x, 0.0) + jnp.log1p(jnp.exp(-jnp.abs(x)))


def _log_sigmoid(x):
    return jnp.minimum(x, 0.0) - jnp.log1p(jnp.exp(-jnp.abs(x)))


def _rms(x, g):
    return x * lax.rsqrt(jnp.mean(x * x, axis=-1, keepdims=True) + EPS) * g


def _norm_mod(x, g, shift, scale):
    return _rms(x, g) * (1.0 + scale) + shift


def _mods_kernel(cc_ref, w_ref, b_ref, o_ref):
    s = _silu(cc_ref[...])
    o_ref[...] = jnp.dot(s, w_ref[...], precision=lax.Precision.HIGHEST,
                         preferred_element_type=F32) + b_ref[...]


def _mods(cc, mod_w, mod_b):
    depth = mod_w.shape[0]
    wb = N_MOD * D // 2
    return pl.pallas_call(
        _mods_kernel,
        grid=(depth, 2),
        in_specs=[pl.BlockSpec((8, D), lambda i, j: (0, 0)),
                  pl.BlockSpec((None, D, wb), lambda i, j: (i, 0, j)),
                  pl.BlockSpec((None, 1, wb), lambda i, j: (i, 0, j))],
        out_specs=pl.BlockSpec((None, 8, wb), lambda i, j: (i, 0, j)),
        out_shape=jax.ShapeDtypeStruct((depth, 8, N_MOD * D), F32),
        compiler_params=_cparams(2),
        name="mods",
    )(cc, mod_w, mod_b.reshape(depth, 1, N_MOD * D))


def _mod_spec(mod_row):
    if mod_row is None:
        return pl.BlockSpec((None, 1, N_MOD * D), lambda b, i: (b, 0, 0))
    return pl.BlockSpec((None, 1, N_MOD * D), lambda b, i: (mod_row, 0, 0))


def _const_spec(shape, index=None):
    index = (0,) * len(shape) if index is None else index
    return pl.BlockSpec(shape, lambda *_: index, pipeline_mode=pl.Buffered(1))


def _ffn_specs(layer):
    return [_const_spec((None, D, 2 * D_FF), (layer, 0, 0)), _const_spec((None, D_FF, D), (layer, 0, 0))]


AB_K, AB_V, AB_LR, AB_Q, AB_R, AB_U, AB_G = 0, 256, 768, 800, 1056, 1568, 2080


def _cast_rows_once(w_ref, wb_ref):
    @pl.when((pl.program_id(0) == 0) & (pl.program_id(1) == 0))
    def _():
        n = w_ref.shape[0]
        for r0 in range(0, n, 512):
            r1 = min(r0 + 512, n)
            wb_ref[r0:r1, :] = w_ref[r0:r1, :].astype(BF16)


def _proj0_kernel(x_ref, mod_ref, g_ref, wt_ref, gw_ref, gb_ref, p_ref, la_ref, wb_ref):
    _cast_rows_once(wt_ref, wb_ref)
    h = _norm_mod(x_ref[...], g_ref[...], mod_ref[:, 0:D], mod_ref[:, D:2 * D]).astype(BF16)
    for dst, src in ((0, AB_V), (512, AB_R), (1024, AB_U), (1536, AB_G)):
        p_ref[:, dst:dst + 512] = _dot_nt(h, wb_ref[src:src + 512, :]).astype(BF16)
    p_ref[:, 2048:2304] = _dot_nt(h, wb_ref[AB_K:AB_K + 256, :]).astype(BF16)
    p_ref[:, 2304:2560] = (_dot_nt(h, wb_ref[AB_Q:AB_Q + 256, :]) * (GLA_DK ** -0.5)).astype(BF16)
    lr = _dot_nt(h, wb_ref[AB_LR:AB_LR + 128, :])
    z = _dot(lr.astype(BF16), gw_ref[...]) + gb_ref[...]
    la_ref[...] = _log_sigmoid(z) * GLA_TAU_INV


def _proj0(x, mods, mod_row, g, w, gw, gb, tm):
    bsz, t, _ = x.shape
    return pl.pallas_call(
        _proj0_kernel,
        grid=(bsz, t // tm),
        in_specs=[pl.BlockSpec((None, tm, D), lambda b, i: (b, i, 0)),
                  _mod_spec(mod_row),
                  _const_spec((1, D)),
                  _const_spec((None,) + w.shape[1:], (0, 0, 0)),
                  _const_spec((128, 512)),
                  _const_spec((1, 512))],
        out_specs=[pl.BlockSpec((None, tm, P0_COLS), lambda b, i: (b, i, 0)),
                   pl.BlockSpec((None, tm, 512), lambda b, i: (b, i, 0))],
        out_shape=[jax.ShapeDtypeStruct((bsz, t, P0_COLS), BF16),
                   jax.ShapeDtypeStruct((bsz, t, 512), F32)],
        scratch_shapes=[pltpu.VMEM(w.shape[1:], BF16)],
        compiler_params=_cparams(2),
        name="proj0",
    )(x, mods, g, w, gw, gb)


def _gla_kernel(*refs):
    ctx_in, lat_in, gng_ref = refs[0:6], refs[6:12], refs[12]
    ctx_a, lat_a, ctx_acc, lat_acc, sf_ref, sb_ref = refs[13:19]
    L = GLA_CHUNK
    row = lax.broadcasted_iota(jnp.int32, (L, L), 0)
    col = lax.broadcasted_iota(jnp.int32, (L, L), 1)
    tril, triu = col <= row, col >= row
    tril_b, triu_b = tril.astype(BF16), triu.astype(BF16)
    bd_mask = (lax.broadcasted_iota(jnp.int32, (256, 128), 0) // GLA_DV
               == lax.broadcasted_iota(jnp.int32, (256, 128), 1) // GLA_DK)
    lane = lax.broadcasted_iota(jnp.int32, (L, 128), 1)
    head_masks = (lane < GLA_DK, lane >= GLA_DK)

    sf_ref[...] = jnp.zeros_like(sf_ref)
    sb_ref[...] = jnp.zeros_like(sb_ref)

    def scan(k_ref, q_ref, v_ref, r_ref, laf_ref, lab_ref, a_ref, o_acc):
        n = k_ref.shape[0] // L

        def one(rows, la_ref, tri, tri_b, last, s_ref, finish):
            k = k_ref[rows, :].astype(F32)
            q = q_ref[rows, :].astype(F32)
            v = v_ref[rows, :]
            b = _dot01_l(tri_b, la_ref[rows, :])
            yield
            b_last = b[last:last + 1, :]
            qd = q * jnp.exp(b)
            ki = (k * jnp.exp(-b)).astype(BF16)
            kd = (k * jnp.exp(b_last - b)).astype(BF16)
            sc = [_dot_nt(jnp.where(head_masks[h], qd, 0.0).astype(BF16), ki) for h in range(2)]
            s_prev = s_ref[...]
            o = _dot_nt(qd.astype(BF16), s_prev.astype(BF16))
            s_ref[...] = s_prev * jnp.exp(b_last) + jnp.where(bd_mask, _dot_tn(v, kd), 0.0)
            yield
            intra = [_dot(jnp.where(tri, sc[h], 0.0).astype(BF16), v[:, h * GLA_DV:(h + 1) * GLA_DV])
                     for h in range(2)]
            o = o + jnp.concatenate(intra, axis=1)
            if not finish:
                o_acc[rows, :] = o
                return
            o = o + o_acc[rows, :]
            r = r_ref[rows, :].astype(F32)
            outs = []
            for h in range(2):
                sl = slice(h * GLA_DV, (h + 1) * GLA_DV)
                outs.append(_rms(o[:, sl], gng_ref[:, sl]) * _silu(r[:, sl]))
            a_ref[rows, :] = jnp.concatenate(outs, axis=1).astype(BF16)

        unroll = 8 if n % 16 == 0 else (4 if n % 8 == 0 else 2)

        def body(finish, iu, carry):
            stages = []
            for u in range(unroll):
                i = iu * unroll + u
                stages.append(one(pl.ds(pl.multiple_of(i * L, L), L), laf_ref, tril, tril_b, L - 1, sf_ref,
                                  finish))
                stages.append(one(pl.ds(pl.multiple_of((n - 1 - i) * L, L), L), lab_ref, triu, triu_b, 0,
                                  sb_ref, finish))
            _lockstep(*stages)
            return carry

        half = n // (2 * unroll)
        lax.fori_loop(0, half, functools.partial(body, False), 0)
        lax.fori_loop(half, 2 * half, functools.partial(body, True), 0)

    scan(*ctx_in, ctx_a, ctx_acc)
    scan(*lat_in, lat_a, lat_acc)


def _gla(pc, lac, p, la, gng):
    bsz, t, _ = p.shape
    tc = pc.shape[1]
    assert (t // GLA_CHUNK) % 4 == 0 and (tc // GLA_CHUNK) % 4 == 0
    npair = GLA_H // 2
    kb, qb = 2048 // 128, 2304 // 128

    def operands(n):
        tok = lambda w, f: pl.BlockSpec((None, n, w), lambda b, h: (b, 0, f(h)))
        return [tok(128, lambda h: kb + h), tok(128, lambda h: qb + h),
                tok(256, lambda h: h), tok(256, lambda h: npair + h),
                tok(128, lambda h: h), tok(128, lambda h: npair + h)]

    out = lambda n: pl.BlockSpec((None, n, 256), lambda b, h: (b, 0, h))
    return pl.pallas_call(
        _gla_kernel,
        grid=(bsz, npair),
        in_specs=operands(tc) + operands(t) + [pl.BlockSpec((1, 256), lambda b, h: (0, h))],
        out_specs=[out(tc), out(t)],
        out_shape=[jax.ShapeDtypeStruct((bsz, tc, 512), BF16), jax.ShapeDtypeStruct((bsz, t, 512), BF16)],
        scratch_shapes=[pltpu.VMEM((tc, 256), F32), pltpu.VMEM((t, 256), F32),
                        pltpu.VMEM((256, 128), F32), pltpu.VMEM((256, 128), F32)],
        compiler_params=_cparams(2),
        name="gla",
    )(pc, pc, pc, pc, lac, lac, p, p, p, p, la, la, gng)


def _ffn_tail(x1, rows, mod_ref, n2g_ref, wi_ref, wo2_ref, act_ref):
    h2 = _norm_mod(x1, n2g_ref[...], mod_ref[:, 3 * D:4 * D], mod_ref[:, 4 * D:5 * D]).astype(BF16)
    yield
    for c0 in range(0, D_FF, 512):
        c1 = min(c0 + 512, D_FF)
        gate = _dot(h2, wi_ref[:, c0:c1])
        up = _dot(h2, wi_ref[:, D_FF + c0:D_FF + c1])
        act_ref[rows, c0:c1] = (_silu(gate) * up).astype(BF16)
    yield
    return x1 + mod_ref[:, 5 * D:6 * D] * _dot(act_ref[rows, :], wo2_ref[...])


def _staggered_halves(tm, part):
    sub = tm // 2 if tm % (2 * GMLP_CHUNK) == 0 else tm

    def delayed(j):
        for _ in range(j):
            yield
        yield from part(slice(j * sub, (j + 1) * sub))

    _lockstep(*[delayed(j) for j in range(tm // sub)])


def _tile_dst(o_ref, out_ref, k):
    return o_ref if k == 1 else out_ref


def _store_transposed(o_ref, out_ref, r, k):
    for j in range(k):
        o_ref[:, j, :] = out_ref[j * r:(j + 1) * r, :]


def _post0_kernel(x_ref, mod_ref, a_ref, u_ref, gv_ref, vng_ref, sw_ref, sbt_ref, wo_ref,
                  n2g_ref, wi_ref, wo2_ref, o_ref, mix_ref, act_ref, out_ref, *, r, k):
    def part(rows):
        uu = jax.nn.gelu(u_ref[rows, :].astype(F32), approximate=True)
        vv = jax.nn.gelu(gv_ref[rows, :].astype(F32), approximate=True)
        mu = jnp.mean(vv, axis=-1, keepdims=True)
        vc = vv - mu
        vn = (vc * lax.rsqrt(jnp.mean(vc * vc, axis=-1, keepdims=True) + EPS) * vng_ref[...]).astype(BF16)
        mix_ref[rows, 0:512] = a_ref[rows, :]
        for ci in range((rows.stop - rows.start) // GMLP_CHUNK):
            rs = slice(ci * GMLP_CHUNK, (ci + 1) * GMLP_CHUNK)
            ms = slice(rows.start + rs.start, rows.start + rs.stop)
            for gi in range(GMLP_G):
                cs = slice(gi * 128, (gi + 1) * 128)
                s = _dot(sw_ref[gi], vn[rs, cs]) + sbt_ref[:, gi:gi + 1]
                mix_ref[ms, 512 + gi * 128:512 + (gi + 1) * 128] = (uu[rs, cs] * s).astype(BF16)
        x1 = x_ref[rows, :] + mod_ref[:, 2 * D:3 * D] * _dot(mix_ref[rows, :], wo_ref[...])
        yield
        dst[rows, :] = yield from _ffn_tail(x1, rows, mod_ref, n2g_ref, wi_ref, wo2_ref, act_ref)

    dst = _tile_dst(o_ref, out_ref, k)
    _staggered_halves(r * k, part)
    if k > 1:
        _store_transposed(o_ref, out_ref, r, k)


def _post1_kernel(x_ref, mod_ref, yn_ref, wo_ref, n2g_ref, wi_ref, wo2_ref, fg_ref,
                  o_ref, act_ref, out_ref, *, r, k):
    def part(rows):
        x1 = x_ref[rows, :] + mod_ref[:, 2 * D:3 * D] * _dot(yn_ref[rows, :], wo_ref[...])
        yield
        x2 = yield from _ffn_tail(x1, rows, mod_ref, n2g_ref, wi_ref, wo2_ref, act_ref)
        dst[rows, :] = _rms(x2, fg_ref[...])

    dst = _tile_dst(o_ref, out_ref, k)
    _staggered_halves(r * k, part)
    if k > 1:
        _store_transposed(o_ref, out_ref, r, k)


def _post0(x, mods, mod_row, a, p, vng, sw, sbt, wo, n2g, wi, wo2, layer, tm, transpose_grid):
    bsz, t, _ = x.shape
    tok = lambda w, j: pl.BlockSpec((None, tm, w), lambda b, i: (b, i, j))
    if transpose_grid:
        k = tm // GRID_W
        out_spec = pl.BlockSpec((None, GRID_W, k, D), lambda b, i: (b, 0, i, 0))
        out_shape = jax.ShapeDtypeStruct((bsz, GRID_W, t // GRID_W, D), F32)
        kern = functools.partial(_post0_kernel, r=GRID_W, k=k)
    else:
        out_spec, out_shape = tok(D, 0), jax.ShapeDtypeStruct((bsz, t, D), F32)
        kern = functools.partial(_post0_kernel, r=tm, k=1)
    return pl.pallas_call(
        kern,
        grid=(bsz, t // tm),
        in_specs=[tok(D, 0), _mod_spec(mod_row), tok(512, 0), tok(512, 2), tok(512, 3),
                  _const_spec((1, 512)), _const_spec((GMLP_G, 128, 128)), _const_spec((128, GMLP_G)),
                  _const_spec((D, D)), _const_spec((1, D))] + _ffn_specs(layer),
        out_specs=out_spec,
        out_shape=out_shape,
        scratch_shapes=[pltpu.VMEM((tm, D), BF16), pltpu.VMEM((tm, D_FF), BF16), pltpu.VMEM((tm, D), F32)],
        compiler_params=_cparams(2),
        name="post0",
    )(x, mods, a, p, p, vng, sw, sbt, wo, n2g, wi, wo2)


def _post1(x, mods, yn, wo, n2g, wi, wo2, layer, fg, tm):
    bsz, t, _ = x.shape
    r = t // GRID_W
    k = tm // r
    tok = lambda w: pl.BlockSpec((None, tm, w), lambda b, i: (b, i, 0))
    return pl.pallas_call(
        functools.partial(_post1_kernel, r=r, k=k),
        grid=(bsz, t // tm),
        in_specs=[tok(D), _mod_spec(None), tok(SSD_INNER),
                  _const_spec((SSD_INNER, D)), _const_spec((1, D))] + _ffn_specs(layer) + [_const_spec((1, D))],
        out_specs=pl.BlockSpec((None, r, k, D), lambda b, i: (b, 0, i, 0)),
        out_shape=jax.ShapeDtypeStruct((bsz, r, GRID_W, D), F32),
        scratch_shapes=[pltpu.VMEM((tm, D_FF), BF16), pltpu.VMEM((tm, D), F32)],
        compiler_params=_cparams(2),
        name="post1",
    )(x, mods, yn, wo, n2g, wi, wo2, fg)


def _proj1_kernel(*refs, state_only):
    if state_only:
        (xm_ref, xp_ref, xn_ref, mod_ref, g_ref, w_ref, cw_ref, cb_ref, dtb_ref,
         xbc_ref, dt_ref, h_ext, p_ext, wb_ref) = refs
    else:
        (xm_ref, xp_ref, xn_ref, mod_ref, g_ref, w_ref, cw_ref, cb_ref, dtb_ref,
         xbc_ref, z_ref, dt_ref, h_ext, p_ext, wb_ref) = refs
    tm = xm_ref.shape[0]
    ncol = xbc_ref.shape[1]
    i = pl.program_id(1)
    last = pl.num_programs(1) - 1

    _cast_rows_once(w_ref, wb_ref)
    z0 = SSD_XBC + 2 * SSD_H
    g = g_ref[...]
    shift, scale = mod_ref[:, 0:D], mod_ref[:, D:2 * D]
    h_ext[0:HALO, :] = jnp.where(i > 0, _norm_mod(xp_ref[...], g, shift, scale), 0.0)
    h_ext[HALO:HALO + tm, :] = _norm_mod(xm_ref[...], g, shift, scale)
    h_ext[HALO + tm:2 * HALO + tm, :] = jnp.where(i < last, _norm_mod(xn_ref[...], g, shift, scale), 0.0)
    hb = h_ext[...].astype(BF16)
    pad = (SSD_CONV - 1) // 2
    nslab = p_ext.shape[0]
    cwid = nslab * 128
    for c0 in range(0, ncol, cwid):
        res = _dot_nt(hb, wb_ref[c0:c0 + cwid, :])
        for s in range(nslab):
            p_ext[s] = res[:, s * 128:(s + 1) * 128]
        for s in range(nslab):
            cs = slice(c0 + s * 128, c0 + (s + 1) * 128)
            acc = cb_ref[:, cs] + cw_ref[0:1, cs] * p_ext[s, HALO - pad:HALO - pad + tm, :]
            for j in range(1, SSD_CONV):
                acc = acc + cw_ref[j:j + 1, cs] * p_ext[s, HALO - pad + j:HALO - pad + j + tm, :]
            xbc_ref[:, cs] = _silu(acc).astype(BF16)
    hm = h_ext[HALO:HALO + tm, :].astype(BF16)
    if not state_only:
        for c0 in range(0, SSD_INNER, 512):
            z_ref[:, c0:c0 + 512] = _dot_nt(hm, wb_ref[z0 + c0:z0 + c0 + 512, :]).astype(BF16)
    dt_ref[...] = _softplus(_dot_nt(hm, wb_ref[SSD_XBC:SSD_XBC + 128, :]) + dtb_ref[...])


def _proj1(x, mods, mod_row, g, w, cw, cb, dtb, tm, state_only):
    bsz, t, _ = x.shape
    hpt = tm // HALO
    tok = lambda n: pl.BlockSpec((None, tm, n), lambda b, i: (b, i, 0))
    shp = lambda n, dt: jax.ShapeDtypeStruct((bsz, t, n), dt)
    ncol = SSD_INNER + SSD_GS if state_only else SSD_XBC
    return pl.pallas_call(
        functools.partial(_proj1_kernel, state_only=state_only),
        grid=(bsz, t // tm),
        in_specs=[tok(D),
                  pl.BlockSpec((None, HALO, D), lambda b, i: (b, jnp.maximum(i * hpt - 1, 0), 0)),
                  pl.BlockSpec((None, HALO, D), lambda b, i: (b, jnp.minimum((i + 1) * hpt, t // HALO - 1), 0)),
                  _mod_spec(mod_row), _const_spec((1, D)), _const_spec((None,) + w.shape[1:], (0, 0, 0)),
                  _const_spec((SSD_CONV, ncol)), _const_spec((1, ncol)), _const_spec((1, 128))],
        out_specs=[tok(ncol)] + ([] if state_only else [tok(SSD_INNER)]) + [tok(128)],
        out_shape=[shp(ncol, BF16)] + ([] if state_only else [shp(SSD_INNER, BF16)]) + [shp(128, F32)],
        scratch_shapes=[pltpu.VMEM((tm + 2 * HALO, D), F32), pltpu.VMEM((4, tm + 2 * HALO, 128), F32),
                        pltpu.VMEM(w.shape[1:], BF16)],
        compiler_params=_cparams(2),
        name="proj1_state" if state_only else "proj1",
    )(x, x, x, mods, g, w, cw, cb, dtb)


def _ssd_kernel(xs_ref, b_ref, c_ref, dt_ref, z_ref, alog_ref, dexp_ref, ng_ref, cxs_ref, cb_ref_, cdt_ref,
                yn_ref, pre_ref, dec_ref, y_acc, sf_ref, sb_ref):
    L = SSD_CHUNK
    n = xs_ref.shape[0] // L
    nc = cxs_ref.shape[0] // L
    npair = SSD_HPG // 2
    grp = pl.program_id(1)
    row = lax.broadcasted_iota(jnp.int32, (L, L), 0)
    col = lax.broadcasted_iota(jnp.int32, (L, L), 1)
    tril, triu = col <= row, col >= row
    tril_b, triu_b = tril.astype(BF16), triu.astype(BF16)
    lane = lax.broadcasted_iota(jnp.int32, (L, 128), 1)
    lo_half = lane < SSD_P
    a_row = -jnp.exp(alog_ref[...]) * LOG2E

    sf_ref[...] = jnp.zeros_like(sf_ref)
    sb_ref[...] = jnp.zeros_like(sb_ref)
    latent = (xs_ref, b_ref, c_ref, dt_ref)
    context = (cxs_ref, cb_ref_, None, cdt_ref)

    def rows_of(c):
        return pl.ds(c * L if isinstance(c, int) else pl.multiple_of(c * L, L), L)

    def prepare(src, c, d, tri_b, last, slot):
        xs_ref, b_ref, c_ref, dt_ref = src
        with_output = c_ref is not None
        rows = rows_of(c)
        dtc = dt_ref[rows, :]
        acum_all = _dot01_l(tri_b, dtc * a_row)
        yield
        shift = lax.rem(128 - d * SSD_H - grp * SSD_HPG, 128)
        acum = pltpu.roll(acum_all, shift, axis=1)
        dtg = pltpu.roll(dtc, shift, axis=1)
        a_last = acum[last:last + 1, :]
        w = jnp.exp2(a_last - acum) * dtg
        pack = jnp.where(lane < SSD_HPG, acum - jnp.log2(dtg), pltpu.roll(w, SSD_HPG, axis=1))
        yield
        bc = b_ref[rows, :]
        pre_ref[slot, d, 0] = acum
        pre_ref[slot, d, 1] = pack.T
        pre_ref[slot, d, 2] = bc.astype(F32).T
        dec_ref[slot, d] = jnp.broadcast_to(jnp.exp2(a_last), (8, 128))
        if with_output:
            pre_ref[slot, d, 3] = _dot_nt(c_ref[rows, :], bc)

    def update(src, c, d, tri, s_ref, finish, slot):
        xs_ref, b_ref, c_ref, dt_ref = src
        with_output = c_ref is not None
        rows = rows_of(c)
        acum_ref, pack_ref, bt_ref, cb_ref = (pre_ref.at[slot, d, j] for j in range(4))
        xs = xs_ref[rows, :]
        if with_output:
            ccf = c_ref[rows, :].astype(F32)
        ys = []
        for hp in range(npair):
            if hp == npair // 2:
                yield
            heads = (2 * hp, 2 * hp + 1)
            xf = xs[:, hp * 128:(hp + 1) * 128].astype(F32)
            xhalf = [jnp.where(lo_half, xf, 0.0).astype(BF16), jnp.where(lo_half, 0.0, xf).astype(BF16)]
            s_pair = s_ref[hp]
            ds = [_dot((bt_ref[...] * pack_ref[SSD_HPG + hh:SSD_HPG + hh + 1, :]).astype(BF16), xhalf[j])
                  for j, hh in enumerate(heads)]
            decs = [jnp.broadcast_to(dec_ref[slot, d, 0:1, hh:hh + 1], (SSD_N, 128)) for hh in heads]
            s_ref[hp] = s_pair * jnp.concatenate(decs, axis=0) + jnp.concatenate(ds, axis=0)
            if not with_output:
                continue
            lhs, clhs = [], []
            for hh in heads:
                acol = jnp.broadcast_to(acum_ref[:, hh:hh + 1], (L, L))
                m = cb_ref[...] * jnp.exp2(jnp.where(tri, acol - pack_ref[hh:hh + 1, :], -1e30))
                lhs.append(m.astype(BF16))
                clhs.append((ccf * jnp.exp2(acol)).astype(BF16))
            rhs = jnp.concatenate(xhalf + [s_pair.astype(BF16)], axis=0)
            ys.append(_dot(jnp.concatenate(lhs + clhs, axis=1), rhs))
        if not with_output:
            return
        yield
        y = jnp.concatenate(ys, axis=1)
        if not finish:
            y_acc[rows, :] = y
            return
        y = y + y_acc[rows, :] + dexp_ref[...] * xs.astype(F32)
        y = y * _silu(z_ref[rows, :].astype(F32))
        yn_ref[rows, :] = _rms(y, ng_ref[...]).astype(BF16)

    unr = 2 if n % 4 == 0 else 1

    def prepares(first, bank):
        out = []
        for u in range(unr):
            i = first + u
            out += [prepare(latent, jnp.minimum(i, n - 1), 0, tril_b, L - 1, bank * unr + u),
                    prepare(latent, jnp.maximum(n - 1 - i, 0), 1, triu_b, 0, bank * unr + u)]
        return out

    def body(finish, iu, carry):
        bank = lax.rem(iu, 2)
        ups = []
        for u in range(unr):
            i = iu * unr + u
            ups += [update(latent, i, 0, tril, sf_ref, finish, bank * unr + u),
                    update(latent, n - 1 - i, 1, triu, sb_ref, finish, bank * unr + u)]
        _lockstep(*(ups + prepares((iu + 1) * unr, 1 - bank)))
        return carry

    for i in range(nc):
        _lockstep(prepare(context, i, 0, tril_b, L - 1, 0), prepare(context, nc - 1 - i, 1, triu_b, 0, 0))
        _lockstep(update(context, i, 0, tril, sf_ref, False, 0),
                  update(context, nc - 1 - i, 1, triu, sb_ref, False, 0))
    _lockstep(*prepares(0, 0))
    half = n // (2 * unr)
    lax.fori_loop(0, half, functools.partial(body, False), 0)
    lax.fori_loop(half, 2 * half, functools.partial(body, True), 0)


def _ssd(xbc, dt, z, cxb, cdt, alog, dexp, ng):
    bsz, t, _ = xbc.shape
    tc = cxb.shape[1]
    assert (t // SSD_CHUNK) % 2 == 0 and tc % SSD_CHUNK == 0
    gw = SSD_HPG * SSD_P
    nb0 = SSD_INNER // SSD_N
    tok = lambda n, w, f: pl.BlockSpec((None, n, w), lambda b, g: (b, 0, f(g)))
    row128 = pl.BlockSpec((1, 128), lambda b, g: (0, 0))
    grow = pl.BlockSpec((1, gw), lambda b, g: (0, g))
    return pl.pallas_call(
        _ssd_kernel,
        grid=(bsz, SSD_G),
        in_specs=[tok(t, gw, lambda g: g), tok(t, SSD_N, lambda g: nb0 + g),
                  tok(t, SSD_N, lambda g: nb0 + SSD_G + g), tok(t, 128, lambda g: 0), tok(t, gw, lambda g: g),
                  row128, grow, grow,
                  tok(tc, gw, lambda g: g), tok(tc, SSD_N, lambda g: nb0 + g), tok(tc, 128, lambda g: 0)],
        out_specs=tok(t, gw, lambda g: g),
        out_shape=jax.ShapeDtypeStruct((bsz, t, SSD_INNER), BF16),
        scratch_shapes=[
            pltpu.VMEM((4, 2, 4, SSD_CHUNK, 128), F32), pltpu.VMEM((4, 2, 8, 128), F32),
            pltpu.VMEM((t, gw), F32),
            pltpu.VMEM((SSD_HPG // 2, 2 * SSD_N, 128), F32),
            pltpu.VMEM((SSD_HPG // 2, 2 * SSD_N, 128), F32)],
        compiler_params=_cparams(2),
        name="ssd",
    )(xbc, xbc, xbc, dt, z, alog, dexp, ng, cxb, cxb, cdt)


def _tile(t, pref):
    return pref if t % pref == 0 else t


def kernel(x, c, ctx, c_ctx, mod_w, mod_b, norm_g, ffn_w_in, ffn_w_out, ab_w_in, ab_gate_w, ab_gate_b,
           ab_gla_norm_g, ab_vnorm_g, ab_spatial_w, ab_spatial_b, ab_w_out, ssd_w_in, ssd_conv_w,
           ssd_conv_b, ssd_dt_bias, ssd_a_log, ssd_d, ssd_norm_g, ssd_w_out, final_norm_g):
    bsz, t, _ = x.shape
    tc = ctx.shape[1]
    assert mod_w.shape[0] == 2 and bsz <= 7 and tc % 128 == 0
    assert t % TOKEN_TILE == 0 and TOKEN_TILE % (8 * GRID_W) == 0 and (TOKEN_TILE * GRID_W) % (8 * t) == 0
    ctx_row = bsz

    cc = jnp.zeros((8, D), F32).at[:bsz].set(c).at[ctx_row].set(c_ctx)
    mods = _mods(cc, mod_w, mod_b)
    m0 = mods[0].reshape(8, 1, N_MOD * D)
    m1 = mods[1].reshape(8, 1, N_MOD * D)

    w0 = jnp.swapaxes(ab_w_in, 1, 2)
    gw = jnp.zeros((128, 512), F32)
    gw = gw.at[0:GLA_LR, 0:256].set(ab_gate_w[0, 0]).at[GLA_LR:2 * GLA_LR, 256:512].set(ab_gate_w[0, 1])
    gw = gw.astype(BF16)
    gb = ab_gate_b[0].reshape(1, 512)
    gng = ab_gla_norm_g[0].reshape(1, 512)
    vng = ab_vnorm_g[0].reshape(1, 512)
    sw = ab_spatial_w[0].astype(BF16)
    sbt = ab_spatial_b[0].T
    wo0 = ab_w_out[0].astype(BF16)
    n1g = [norm_g[i, 0].reshape(1, D) for i in range(2)]
    n2g = [norm_g[i, 1].reshape(1, D) for i in range(2)]
    wi = ffn_w_in.astype(BF16)
    wo2 = ffn_w_out.astype(BF16)

    tmc = _tile(tc, 256)
    tmx = TOKEN_TILE
    nctx = bsz * tc
    tmf = _tile(nctx, TOKEN_TILE)
    ctxf = ctx.reshape(1, nctx, D)
    pcf, lacf = _proj0(ctxf, m0, ctx_row, n1g[0], w0, gw, gb, tmf)
    px, lax_ = _proj0(x, m0, None, n1g[0], w0, gw, gb, tmx)
    ac, ax = _gla(pcf.reshape(bsz, tc, P0_COLS), lacf.reshape(bsz, tc, 512), px, lax_, gng)
    ctx1 = _post0(ctxf, m0, ctx_row, ac.reshape(1, nctx, 512), pcf, vng, sw, sbt, wo0, n2g[0], wi, wo2, 0,
                  tmf, False).reshape(bsz, tc, D)
    x1 = _post0(x, m0, None, ax, px, vng, sw, sbt, wo0, n2g[0], wi, wo2, 0, tmx, True)
    x1 = x1.reshape(bsz, t, D)

    w1 = jnp.swapaxes(ssd_w_in, 1, 2)
    cw = ssd_conv_w[0]
    cb = ssd_conv_b[0].reshape(1, SSD_XBC)
    dtb = jnp.concatenate([ssd_dt_bias[0].reshape(1, 2 * SSD_H), jnp.zeros((1, 64), F32)], axis=1)
    alog = jnp.concatenate([ssd_a_log[0].reshape(1, 2 * SSD_H), jnp.zeros((1, 64), F32)], axis=1)
    dexp = jnp.repeat(ssd_d[0], SSD_P).reshape(1, SSD_INNER)
    sng = ssd_norm_g[0].reshape(1, SSD_INNER)
    wo1 = ssd_w_out[0].astype(BF16)

    xb_c, dt_c = _proj1(ctx1, m1, ctx_row, n1g[1], w1, cw, cb, dtb, tmc, True)
    xbc_x, z_x, dt_x = _proj1(x1, m1, None, n1g[1], w1, cw, cb, dtb, tmx, False)
    yn = _ssd(xbc_x, dt_x, z_x, xb_c, dt_c, alog, dexp, sng)
    out = _post1(x1, m1, yn, wo1, n2g[1], wi, wo2, 1, final_norm_g.reshape(1, D), tmx)
    return out.reshape(bsz, t, D)
```

```python
import functools
import itertools

import jax
import jax.numpy as jnp
from jax import lax
from jax.experimental import pallas as pl
from jax.experimental.pallas import tpu as pltpu

F32 = jnp.float32
BF16 = jnp.bfloat16

D = 1024
N_MOD = 6
EPS = 1e-6
LOG2E = 1.4426950408889634
D_FF = 2816
GRID_W = 64

GLA_H, GLA_DK, GLA_DV, GLA_LR, GLA_CHUNK = 4, 64, 128, 16, 64
GLA_TAU_INV = 1.0 / 16.0
GMLP_W, GMLP_G, GMLP_CHUNK = 512, 4, 128
P0_COLS = 2560
P0_W = P0_COLS + 128

SSD_INNER, SSD_P, SSD_H, SSD_G, SSD_HPG, SSD_N, SSD_CHUNK, SSD_CONV = 2048, 64, 32, 4, 8, 128, 128, 5
SSD_GS = SSD_G * SSD_N
SSD_XBC = SSD_INNER + 2 * SSD_GS
HALO = 8
TOKEN_TILE = 512

VMEM_LIMIT = 56 * 1024 * 1024


def _cparams(n_axes):
    return pltpu.CompilerParams(dimension_semantics=("arbitrary",) * n_axes,
                                vmem_limit_bytes=VMEM_LIMIT)


def _dot(a, b):
    return jnp.dot(a, b, preferred_element_type=F32)


def _dot_nt(a, b):
    return lax.dot_general(a, b, (((1,), (1,)), ((), ())), preferred_element_type=F32)


def _dot_tn(a, b):
    return lax.dot_general(a, b, (((0,), (0,)), ((), ())), preferred_element_type=F32)


def _split3(x):
    hi = x.astype(BF16)
    r1 = x - hi.astype(F32)
    mid = r1.astype(BF16)
    lo = (r1 - mid.astype(F32)).astype(BF16)
    return hi, mid, lo


def _dot01_l(t01, x):
    hi, mid, lo = _split3(x)
    return (_dot(t01, lo) + _dot(t01, mid)) + _dot(t01, hi)


def _lockstep(*stages):
    for _ in itertools.zip_longest(*stages):
        pass


def _silu(x):
    h = 0.5 * x
    return h * jnp.tanh(h) + h


def _softplus(x):
    return jnp.maximum(x, 0.0) + jnp.log1p(jnp.exp(-jnp.abs(x)))


def _log_sigmoid(x):
    return jnp.minimum(x, 0.0) - jnp.log1p(jnp.exp(-jnp.abs(x)))


def _rms(x, g):
    return x * lax.rsqrt(jnp.mean(x * x, axis=-1, keepdims=True) + EPS) * g


def _norm_mod(x, g, shift, scale):
    return _rms(x, g) * (1.0 + scale) + shift


def _mods_kernel(cc_ref, w_ref, b_ref, o_ref):
    s = _silu(cc_ref[...])
    o_ref[...] = jnp.dot(s, w_ref[...], precision=lax.Precision.HIGHEST,
                         preferred_element_type=F32) + b_ref[...]


def _mods(cc, mod_w, mod_b):
    depth = mod_w.shape[0]
    wb = N_MOD * D // 2
    return pl.pallas_call(
        _mods_kernel,
        grid=(depth, 2),
        in_specs=[pl.BlockSpec((8, D), lambda i, j: (0, 0)),
                  pl.BlockSpec((None, D, wb), lambda i, j: (i, 0, j)),
                  pl.BlockSpec((None, 1, wb), lambda i, j: (i, 0, j))],
        out_specs=pl.BlockSpec((None, 8, wb), lambda i, j: (i, 0, j)),
        out_shape=jax.ShapeDtypeStruct((depth, 8, N_MOD * D), F32),
        compiler_params=_cparams(2),
        name="mods",
    )(cc, mod_w, mod_b.reshape(depth, 1, N_MOD * D))


def _mod_spec(mod_row):
    if mod_row is None:
        return pl.BlockSpec((None, 1, N_MOD * D), lambda b, i: (b, 0, 0))
    return pl.BlockSpec((None, 1, N_MOD * D), lambda b, i: (mod_row, 0, 0))


def _const_spec(shape, index=None):
    index = (0,) * len(shape) if index is None else index
    return pl.BlockSpec(shape, lambda *_: index, pipeline_mode=pl.Buffered(1))


def _ffn_specs(layer):
    return [_const_spec((None, D, 2 * D_FF), (layer, 0, 0)), _const_spec((None, D_FF, D), (layer, 0, 0))]


AB_K, AB_V, AB_LR, AB_Q, AB_R, AB_U, AB_G = 0, 256, 768, 800, 1056, 1568, 2080


def _cast_rows_once(w_ref, wb_ref):
    @pl.when((pl.program_id(0) == 0) & (pl.program_id(1) == 0))
    def _():
        n = w_ref.shape[0]
        for r0 in range(0, n, 512):
            r1 = min(r0 + 512, n)
            wb_ref[r0:r1, :] = w_ref[r0:r1, :].astype(BF16)


def _proj0_kernel(x_ref, mod_ref, g_ref, wt_ref, gw_ref, gb_ref, p_ref, la_ref, wb_ref):
    _cast_rows_once(wt_ref, wb_ref)
    h = _norm_mod(x_ref[...], g_ref[...], mod_ref[:, 0:D], mod_ref[:, D:2 * D]).astype(BF16)
    for dst, src in ((0, AB_V), (512, AB_R), (1024, AB_U), (1536, AB_G)):
        p_ref[:, dst:dst + 512] = _dot_nt(h, wb_ref[src:src + 512, :]).astype(BF16)
    p_ref[:, 2048:2304] = _dot_nt(h, wb_ref[AB_K:AB_K + 256, :]).astype(BF16)
    p_ref[:, 2304:2560] = (_dot_nt(h, wb_ref[AB_Q:AB_Q + 256, :]) * (GLA_DK ** -0.5)).astype(BF16)
    lr = _dot_nt(h, wb_ref[AB_LR:AB_LR + 128, :])
    z = _dot(lr.astype(BF16), gw_ref[...]) + gb_ref[...]
    la_ref[...] = _log_sigmoid(z) * GLA_TAU_INV


def _proj0(x, mods, mod_row, g, w, gw, gb, tm):
    bsz, t, _ = x.shape
    return pl.pallas_call(
        _proj0_kernel,
        grid=(bsz, t // tm),
        in_specs=[pl.BlockSpec((None, tm, D), lambda b, i: (b, i, 0)),
                  _mod_spec(mod_row),
                  _const_spec((1, D)),
                  _const_spec((None,) + w.shape[1:], (0, 0, 0)),
                  _const_spec((128, 512)),
                  _const_spec((1, 512))],
        out_specs=[pl.BlockSpec((None, tm, P0_COLS), lambda b, i: (b, i, 0)),
                   pl.BlockSpec((None, tm, 512), lambda b, i: (b, i, 0))],
        out_shape=[jax.ShapeDtypeStruct((bsz, t, P0_COLS), BF16),
                   jax.ShapeDtypeStruct((bsz, t, 512), F32)],
        scratch_shapes=[pltpu.VMEM(w.shape[1:], BF16)],
        compiler_params=_cparams(2),
        name="proj0",
    )(x, mods, g, w, gw, gb)


def _gla_kernel(*refs):
    ctx_in, lat_in, gng_ref = refs[0:6], refs[6:12], refs[12]
    ctx_a, lat_a, ctx_acc, lat_acc, sf_ref, sb_ref = refs[13:19]
    L = GLA_CHUNK
    row = lax.broadcasted_iota(jnp.int32, (L, L), 0)
    col = lax.broadcasted_iota(jnp.int32, (L, L), 1)
    tril, triu = col <= row, col >= row
    tril_b, triu_b = tril.astype(BF16), triu.astype(BF16)
    bd_mask = (lax.broadcasted_iota(jnp.int32, (256, 128), 0) // GLA_DV
               == lax.broadcasted_iota(jnp.int32, (256, 128), 1) // GLA_DK)
    lane = lax.broadcasted_iota(jnp.int32, (L, 128), 1)
    head_masks = (lane < GLA_DK, lane >= GLA_DK)

    sf_ref[...] = jnp.zeros_like(sf_ref)
    sb_ref[...] = jnp.zeros_like(sb_ref)

    def scan(k_ref, q_ref, v_ref, r_ref, laf_ref, lab_ref, a_ref, o_acc):
        n = k_ref.shape[0] // L

        def one(rows, la_ref, tri, tri_b, last, s_ref, finish):
            k = k_ref[rows, :].astype(F32)
            q = q_ref[rows, :].astype(F32)
            v = v_ref[rows, :]
            b = _dot01_l(tri_b, la_ref[rows, :])
            yield
            b_last = b[last:last + 1, :]
            qd = q * jnp.exp(b)
            ki = (k * jnp.exp(-b)).astype(BF16)
            kd = (k * jnp.exp(b_last - b)).astype(BF16)
            sc = [_dot_nt(jnp.where(head_masks[h], qd, 0.0).astype(BF16), ki) for h in range(2)]
            s_prev = s_ref[...]
            o = _dot_nt(qd.astype(BF16), s_prev.astype(BF16))
            s_ref[...] = s_prev * jnp.exp(b_last) + jnp.where(bd_mask, _dot_tn(v, kd), 0.0)
            yield
            intra = [_dot(jnp.where(tri, sc[h], 0.0).astype(BF16), v[:, h * GLA_DV:(h + 1) * GLA_DV])
                     for h in range(2)]
            o = o + jnp.concatenate(intra, axis=1)
            if not finish:
                o_acc[rows, :] = o
                return
            o = o + o_acc[rows, :]
            r = r_ref[rows, :].astype(F32)
            outs = []
            for h in range(2):
                sl = slice(h * GLA_DV, (h + 1) * GLA_DV)
                outs.append(_rms(o[:, sl], gng_ref[:, sl]) * _silu(r[:, sl]))
            a_ref[rows, :] = jnp.concatenate(outs, axis=1).astype(BF16)

        unroll = 16 if n % 32 == 0 else (4 if n % 8 == 0 else 2)

        def body(finish, iu, carry):
            stages = []
            for u in range(unroll):
                i = iu * unroll + u
                stages.append(one(pl.ds(pl.multiple_of(i * L, L), L), laf_ref, tril, tril_b, L - 1, sf_ref,
                                  finish))
                stages.append(one(pl.ds(pl.multiple_of((n - 1 - i) * L, L), L), lab_ref, triu, triu_b, 0,
                                  sb_ref, finish))
            _lockstep(*stages)
            return carry

        half = n // (2 * unroll)
        lax.fori_loop(0, half, functools.partial(body, False), 0)
        lax.fori_loop(half, 2 * half, functools.partial(body, True), 0)

    scan(*ctx_in, ctx_a, ctx_acc)
    scan(*lat_in, lat_a, lat_acc)


def _gla(pc, lac, p, la, gng):
    bsz, t, _ = p.shape
    tc = pc.shape[1]
    assert (t // GLA_CHUNK) % 4 == 0 and (tc // GLA_CHUNK) % 4 == 0
    npair = GLA_H // 2
    kb, qb = 2048 // 128, 2304 // 128

    def operands(n):
        tok = lambda w, f: pl.BlockSpec((None, n, w), lambda b, h: (b, 0, f(h)))
        return [tok(128, lambda h: kb + h), tok(128, lambda h: qb + h),
                tok(256, lambda h: h), tok(256, lambda h: npair + h),
                tok(128, lambda h: h), tok(128, lambda h: npair + h)]

    out = lambda n: pl.BlockSpec((None, n, 256), lambda b, h: (b, 0, h))
    return pl.pallas_call(
        _gla_kernel,
        grid=(bsz, npair),
        in_specs=operands(tc) + operands(t) + [pl.BlockSpec((1, 256), lambda b, h: (0, h))],
        out_specs=[out(tc), out(t)],
        out_shape=[jax.ShapeDtypeStruct((bsz, tc, 512), BF16), jax.ShapeDtypeStruct((bsz, t, 512), BF16)],
        scratch_shapes=[pltpu.VMEM((tc, 256), F32), pltpu.VMEM((t, 256), F32),
                        pltpu.VMEM((256, 128), F32), pltpu.VMEM((256, 128), F32)],
        compiler_params=_cparams(2),
        name="gla",
    )(pc, pc, pc, pc, lac, lac, p, p, p, p, la, la, gng)


def _ffn_tail(x1, rows, mod_ref, n2g_ref, wi_ref, wo2_ref, act_ref):
    h2 = _norm_mod(x1, n2g_ref[...], mod_ref[:, 3 * D:4 * D], mod_ref[:, 4 * D:5 * D]).astype(BF16)
    yield
    for c0 in range(0, D_FF, 512):
        c1 = min(c0 + 512, D_FF)
        gate = _dot(h2, wi_ref[:, c0:c1])
        up = _dot(h2, wi_ref[:, D_FF + c0:D_FF + c1])
        act_ref[rows, c0:c1] = (_silu(gate) * up).astype(BF16)
    yield
    return x1 + mod_ref[:, 5 * D:6 * D] * _dot(act_ref[rows, :], wo2_ref[...])


def _staggered_halves(tm, part):
    sub = tm // 2 if tm % (2 * GMLP_CHUNK) == 0 else tm

    def delayed(j):
        for _ in range(j):
            yield
        yield from part(slice(j * sub, (j + 1) * sub))

    _lockstep(*[delayed(j) for j in range(tm // sub)])


def _tile_dst(o_ref, out_ref, k):
    return o_ref if k == 1 else out_ref


def _store_transposed(o_ref, out_ref, r, k):
    for j in range(k):
        o_ref[:, j, :] = out_ref[j * r:(j + 1) * r, :]


def _post0_kernel(x_ref, mod_ref, a_ref, u_ref, gv_ref, vng_ref, sw_ref, sbt_ref, wo_ref,
                  n2g_ref, wi_ref, wo2_ref, o_ref, mix_ref, act_ref, out_ref, *, r, k):
    def part(rows):
        uu = jax.nn.gelu(u_ref[rows, :].astype(F32), approximate=True)
        vv = jax.nn.gelu(gv_ref[rows, :].astype(F32), approximate=True)
        mu = jnp.mean(vv, axis=-1, keepdims=True)
        vc = vv - mu
        vn = (vc * lax.rsqrt(jnp.mean(vc * vc, axis=-1, keepdims=True) + EPS) * vng_ref[...]).astype(BF16)
        mix_ref[rows, 0:512] = a_ref[rows, :]
        for ci in range((rows.stop - rows.start) // GMLP_CHUNK):
            rs = slice(ci * GMLP_CHUNK, (ci + 1) * GMLP_CHUNK)
            ms = slice(rows.start + rs.start, rows.start + rs.stop)
            for gi in range(GMLP_G):
                cs = slice(gi * 128, (gi + 1) * 128)
                s = _dot(sw_ref[gi], vn[rs, cs]) + sbt_ref[:, gi:gi + 1]
                mix_ref[ms, 512 + gi * 128:512 + (gi + 1) * 128] = (uu[rs, cs] * s).astype(BF16)
        x1 = x_ref[rows, :] + mod_ref[:, 2 * D:3 * D] * _dot(mix_ref[rows, :], wo_ref[...])
        yield
        dst[rows, :] = yield from _ffn_tail(x1, rows, mod_ref, n2g_ref, wi_ref, wo2_ref, act_ref)

    dst = _tile_dst(o_ref, out_ref, k)
    _staggered_halves(r * k, part)
    if k > 1:
        _store_transposed(o_ref, out_ref, r, k)


def _post1_kernel(x_ref, mod_ref, yn_ref, wo_ref, n2g_ref, wi_ref, wo2_ref, fg_ref,
                  o_ref, act_ref, out_ref, *, r, k):
    def part(rows):
        x1 = x_ref[rows, :] + mod_ref[:, 2 * D:3 * D] * _dot(yn_ref[rows, :], wo_ref[...])
        yield
        x2 = yield from _ffn_tail(x1, rows, mod_ref, n2g_ref, wi_ref, wo2_ref, act_ref)
        dst[rows, :] = _rms(x2, fg_ref[...])

    dst = _tile_dst(o_ref, out_ref, k)
    _staggered_halves(r * k, part)
    if k > 1:
        _store_transposed(o_ref, out_ref, r, k)


def _post0(x, mods, mod_row, a, p, vng, sw, sbt, wo, n2g, wi, wo2, layer, tm, transpose_grid):
    bsz, t, _ = x.shape
    tok = lambda w, j: pl.BlockSpec((None, tm, w), lambda b, i: (b, i, j))
    if transpose_grid:
        k = tm // GRID_W
        out_spec = pl.BlockSpec((None, GRID_W, k, D), lambda b, i: (b, 0, i, 0))
        out_shape = jax.ShapeDtypeStruct((bsz, GRID_W, t // GRID_W, D), F32)
        kern = functools.partial(_post0_kernel, r=GRID_W, k=k)
    else:
        out_spec, out_shape = tok(D, 0), jax.ShapeDtypeStruct((bsz, t, D), F32)
        kern = functools.partial(_post0_kernel, r=tm, k=1)
    return pl.pallas_call(
        kern,
        grid=(bsz, t // tm),
        in_specs=[tok(D, 0), _mod_spec(mod_row), tok(512, 0), tok(512, 2), tok(512, 3),
                  _const_spec((1, 512)), _const_spec((GMLP_G, 128, 128)), _const_spec((128, GMLP_G)),
                  _const_spec((D, D)), _const_spec((1, D))] + _ffn_specs(layer),
        out_specs=out_spec,
        out_shape=out_shape,
        scratch_shapes=[pltpu.VMEM((tm, D), BF16), pltpu.VMEM((tm, D_FF), BF16), pltpu.VMEM((tm, D), F32)],
        compiler_params=_cparams(2),
        name="post0",
    )(x, mods, a, p, p, vng, sw, sbt, wo, n2g, wi, wo2)


def _post1(x, mods, yn, wo, n2g, wi, wo2, layer, fg, tm):
    bsz, t, _ = x.shape
    r = t // GRID_W
    k = tm // r
    tok = lambda w: pl.BlockSpec((None, tm, w), lambda b, i: (b, i, 0))
    return pl.pallas_call(
        functools.partial(_post1_kernel, r=r, k=k),
        grid=(bsz, t // tm),
        in_specs=[tok(D), _mod_spec(None), tok(SSD_INNER),
                  _const_spec((SSD_INNER, D)), _const_spec((1, D))] + _ffn_specs(layer) + [_const_spec((1, D))],
        out_specs=pl.BlockSpec((None, r, k, D), lambda b, i: (b, 0, i, 0)),
        out_shape=jax.ShapeDtypeStruct((bsz, r, GRID_W, D), F32),
        scratch_shapes=[pltpu.VMEM((tm, D_FF), BF16), pltpu.VMEM((tm, D), F32)],
        compiler_params=_cparams(2),
        name="post1",
    )(x, mods, yn, wo, n2g, wi, wo2, fg)


def _proj1_kernel(*refs, state_only):
    if state_only:
        (xm_ref, xp_ref, xn_ref, mod_ref, g_ref, w_ref, cw_ref, cb_ref, dtb_ref,
         xbc_ref, dt_ref, h_ext, p_ext, wb_ref) = refs
    else:
        (xm_ref, xp_ref, xn_ref, mod_ref, g_ref, w_ref, cw_ref, cb_ref, dtb_ref,
         xbc_ref, z_ref, dt_ref, h_ext, p_ext, wb_ref) = refs
    tm = xm_ref.shape[0]
    ncol = xbc_ref.shape[1]
    i = pl.program_id(1)
    last = pl.num_programs(1) - 1

    _cast_rows_once(w_ref, wb_ref)
    z0 = SSD_XBC + 2 * SSD_H
    g = g_ref[...]
    shift, scale = mod_ref[:, 0:D], mod_ref[:, D:2 * D]
    h_ext[0:HALO, :] = jnp.where(i > 0, _norm_mod(xp_ref[...], g, shift, scale), 0.0)
    h_ext[HALO:HALO + tm, :] = _norm_mod(xm_ref[...], g, shift, scale)
    h_ext[HALO + tm:2 * HALO + tm, :] = jnp.where(i < last, _norm_mod(xn_ref[...], g, shift, scale), 0.0)
    hb = h_ext[...].astype(BF16)
    pad = (SSD_CONV - 1) // 2
    nslab = p_ext.shape[0]
    cwid = nslab * 128
    for c0 in range(0, ncol, cwid):
        res = _dot_nt(hb, wb_ref[c0:c0 + cwid, :])
        for s in range(nslab):
            p_ext[s] = res[:, s * 128:(s + 1) * 128]
        for s in range(nslab):
            cs = slice(c0 + s * 128, c0 + (s + 1) * 128)
            acc = cb_ref[:, cs] + cw_ref[0:1, cs] * p_ext[s, HALO - pad:HALO - pad + tm, :]
            for j in range(1, SSD_CONV):
                acc = acc + cw_ref[j:j + 1, cs] * p_ext[s, HALO - pad + j:HALO - pad + j + tm, :]
            xbc_ref[:, cs] = _silu(acc).astype(BF16)
    hm = h_ext[HALO:HALO + tm, :].astype(BF16)
    if not state_only:
        for c0 in range(0, SSD_INNER, 512):
            z_ref[:, c0:c0 + 512] = _dot_nt(hm, wb_ref[z0 + c0:z0 + c0 + 512, :]).astype(BF16)
    dt_ref[...] = _softplus(_dot_nt(hm, wb_ref[SSD_XBC:SSD_XBC + 128, :]) + dtb_ref[...])


def _proj1(x, mods, mod_row, g, w, cw, cb, dtb, tm, state_only):
    bsz, t, _ = x.shape
    hpt = tm // HALO
    tok = lambda n: pl.BlockSpec((None, tm, n), lambda b, i: (b, i, 0))
    shp = lambda n, dt: jax.ShapeDtypeStruct((bsz, t, n), dt)
    ncol = SSD_INNER + SSD_GS if state_only else SSD_XBC
    wrows = SSD_XBC + 128 if state_only else w.shape[1]
    return pl.pallas_call(
        functools.partial(_proj1_kernel, state_only=state_only),
        grid=(bsz, t // tm),
        in_specs=[tok(D),
                  pl.BlockSpec((None, HALO, D), lambda b, i: (b, jnp.maximum(i * hpt - 1, 0), 0)),
                  pl.BlockSpec((None, HALO, D), lambda b, i: (b, jnp.minimum((i + 1) * hpt, t // HALO - 1), 0)),
                  _mod_spec(mod_row), _const_spec((1, D)), _const_spec((None, wrows, D), (0, 0, 0)),
                  _const_spec((SSD_CONV, ncol)), _const_spec((1, ncol)), _const_spec((1, 128))],
        out_specs=[tok(ncol)] + ([] if state_only else [tok(SSD_INNER)]) + [tok(128)],
        out_shape=[shp(ncol, BF16)] + ([] if state_only else [shp(SSD_INNER, BF16)]) + [shp(128, F32)],
        scratch_shapes=[pltpu.VMEM((tm + 2 * HALO, D), F32), pltpu.VMEM((4, tm + 2 * HALO, 128), F32),
                        pltpu.VMEM((wrows, D), BF16)],
        compiler_params=_cparams(2),
        name="proj1_state" if state_only else "proj1",
    )(x, x, x, mods, g, w, cw, cb, dtb)


def _ssd_kernel(xs_ref, b_ref, c_ref, dt_ref, z_ref, alog_ref, dexp_ref, ng_ref, cxs_ref, cb_ref_, cdt_ref,
                yn_ref, pre_ref, dec_ref, y_acc, sf_ref, sb_ref):
    L = SSD_CHUNK
    n = xs_ref.shape[0] // L
    nc = cxs_ref.shape[0] // L
    npair = SSD_HPG // 2
    grp = pl.program_id(1)
    row = lax.broadcasted_iota(jnp.int32, (L, L), 0)
    col = lax.broadcasted_iota(jnp.int32, (L, L), 1)
    tril, triu = col <= row, col >= row
    tril_b, triu_b = tril.astype(BF16), triu.astype(BF16)
    lane = lax.broadcasted_iota(jnp.int32, (L, 128), 1)
    lo_half = lane < SSD_P
    a_row = -jnp.exp(alog_ref[...]) * LOG2E

    sf_ref[...] = jnp.zeros_like(sf_ref)
    sb_ref[...] = jnp.zeros_like(sb_ref)
    latent = (xs_ref, b_ref, c_ref, dt_ref)
    context = (cxs_ref, cb_ref_, None, cdt_ref)

    def rows_of(c):
        return pl.ds(c * L if isinstance(c, int) else pl.multiple_of(c * L, L), L)

    def prepare(src, c, d, tri_b, last, slot):
        xs_ref, b_ref, c_ref, dt_ref = src
        with_output = c_ref is not None
        rows = rows_of(c)
        dtc = dt_ref[rows, :]
        acum_all = _dot01_l(tri_b, dtc * a_row)
        yield
        shift = lax.rem(128 - d * SSD_H - grp * SSD_HPG, 128)
        acum = pltpu.roll(acum_all, shift, axis=1)
        dtg = pltpu.roll(dtc, shift, axis=1)
        a_last = acum[last:last + 1, :]
        w = jnp.exp2(a_last - acum) * dtg
        pack = jnp.where(lane < SSD_HPG, acum - jnp.log2(dtg), pltpu.roll(w, SSD_HPG, axis=1))
        yield
        bc = b_ref[rows, :]
        pre_ref[slot, d, 0] = acum
        pre_ref[slot, d, 1] = pack.T
        pre_ref[slot, d, 2] = bc.astype(F32).T
        dec_ref[slot, d] = jnp.broadcast_to(jnp.exp2(a_last), (8, 128))
        if with_output:
            pre_ref[slot, d, 3] = _dot_nt(c_ref[rows, :], bc)

    def update(src, c, d, tri, s_ref, finish, slot):
        xs_ref, b_ref, c_ref, dt_ref = src
        with_output = c_ref is not None
        rows = rows_of(c)
        acum_ref, pack_ref, bt_ref, cb_ref = (pre_ref.at[slot, d, j] for j in range(4))
        xs = xs_ref[rows, :]
        if with_output:
            ccf = c_ref[rows, :].astype(F32)
        ys = []
        for hp in range(npair):
            if hp == npair // 2:
                yield
            heads = (2 * hp, 2 * hp + 1)
            xf = xs[:, hp * 128:(hp + 1) * 128].astype(F32)
            xhalf = [jnp.where(lo_half, xf, 0.0).astype(BF16), jnp.where(lo_half, 0.0, xf).astype(BF16)]
            s_pair = s_ref[hp]
            ds = [_dot((bt_ref[...] * pack_ref[SSD_HPG + hh:SSD_HPG + hh + 1, :]).astype(BF16), xhalf[j])
                  for j, hh in enumerate(heads)]
            decs = [jnp.broadcast_to(dec_ref[slot, d, 0:1, hh:hh + 1], (SSD_N, 128)) for hh in heads]
            s_ref[hp] = s_pair * jnp.concatenate(decs, axis=0) + jnp.concatenate(ds, axis=0)
            if not with_output:
                continue
            lhs, clhs = [], []
            for hh in heads:
                acol = jnp.broadcast_to(acum_ref[:, hh:hh + 1], (L, L))
                m = cb_ref[...] * jnp.exp2(jnp.where(tri, acol - pack_ref[hh:hh + 1, :], -1e30))
                lhs.append(m.astype(BF16))
                clhs.append((ccf * jnp.exp2(acol)).astype(BF16))
            rhs = jnp.concatenate(xhalf + [s_pair.astype(BF16)], axis=0)
            ys.append(_dot(jnp.concatenate(lhs + clhs, axis=1), rhs))
        if not with_output:
            return
        yield
        y = jnp.concatenate(ys, axis=1)
        if not finish:
            y_acc[rows, :] = y
            return
        y = y + y_acc[rows, :] + dexp_ref[...] * xs.astype(F32)
        y = y * _silu(z_ref[rows, :].astype(F32))
        yn_ref[rows, :] = _rms(y, ng_ref[...]).astype(BF16)

    unr = pre_ref.shape[0] // 2

    def prepares(first, bank):
        out = []
        for u in range(unr):
            i = first + u
            out += [prepare(latent, jnp.minimum(i, n - 1), 0, tril_b, L - 1, bank * unr + u),
                    prepare(latent, jnp.maximum(n - 1 - i, 0), 1, triu_b, 0, bank * unr + u)]
        return out

    def body(finish, iu, carry):
        bank = lax.rem(iu, 2)
        ups = []
        for u in range(unr):
            i = iu * unr + u
            ups += [update(latent, i, 0, tril, sf_ref, finish, bank * unr + u),
                    update(latent, n - 1 - i, 1, triu, sb_ref, finish, bank * unr + u)]
        _lockstep(*(ups + prepares((iu + 1) * unr, 1 - bank)))
        return carry

    for i in range(nc):
        _lockstep(prepare(context, i, 0, tril_b, L - 1, 0), prepare(context, nc - 1 - i, 1, triu_b, 0, 0))
        _lockstep(update(context, i, 0, tril, sf_ref, False, 0),
                  update(context, nc - 1 - i, 1, triu, sb_ref, False, 0))
    _lockstep(*prepares(0, 0))
    half = n // (2 * unr)
    lax.fori_loop(0, half, functools.partial(body, False), 0)
    lax.fori_loop(half, 2 * half, functools.partial(body, True), 0)


def _ssd(xbc, dt, z, cxb, cdt, alog, dexp, ng):
    bsz, t, _ = xbc.shape
    tc = cxb.shape[1]
    assert (t // SSD_CHUNK) % 2 == 0 and tc % SSD_CHUNK == 0
    n = t // SSD_CHUNK
    unr = 4 if n % 8 == 0 else (2 if n % 4 == 0 else 1)
    gw = SSD_HPG * SSD_P
    nb0 = SSD_INNER // SSD_N
    tok = lambda n, w, f: pl.BlockSpec((None, n, w), lambda b, g: (b, 0, f(g)))
    row128 = pl.BlockSpec((1, 128), lambda b, g: (0, 0))
    grow = pl.BlockSpec((1, gw), lambda b, g: (0, g))
    return pl.pallas_call(
        _ssd_kernel,
        grid=(bsz, SSD_G),
        in_specs=[tok(t, gw, lambda g: g), tok(t, SSD_N, lambda g: nb0 + g),
                  tok(t, SSD_N, lambda g: nb0 + SSD_G + g), tok(t, 128, lambda g: 0), tok(t, gw, lambda g: g),
                  row128, grow, grow,
                  tok(tc, gw, lambda g: g), tok(tc, SSD_N, lambda g: nb0 + g), tok(tc, 128, lambda g: 0)],
        out_specs=tok(t, gw, lambda g: g),
        out_shape=jax.ShapeDtypeStruct((bsz, t, SSD_INNER), BF16),
        scratch_shapes=[
            pltpu.VMEM((2 * unr, 2, 4, SSD_CHUNK, 128), F32), pltpu.VMEM((2 * unr, 2, 8, 128), F32),
            pltpu.VMEM((t, gw), F32),
            pltpu.VMEM((SSD_HPG // 2, 2 * SSD_N, 128), F32),
            pltpu.VMEM((SSD_HPG // 2, 2 * SSD_N, 128), F32)],
        compiler_params=_cparams(2),
        name="ssd",
    )(xbc, xbc, xbc, dt, z, alog, dexp, ng, cxb, cxb, cdt)


def _tile(t, pref):
    return pref if t % pref == 0 else t


def kernel(x, c, ctx, c_ctx, mod_w, mod_b, norm_g, ffn_w_in, ffn_w_out, ab_w_in, ab_gate_w, ab_gate_b,
           ab_gla_norm_g, ab_vnorm_g, ab_spatial_w, ab_spatial_b, ab_w_out, ssd_w_in, ssd_conv_w,
           ssd_conv_b, ssd_dt_bias, ssd_a_log, ssd_d, ssd_norm_g, ssd_w_out, final_norm_g):
    bsz, t, _ = x.shape
    tc = ctx.shape[1]
    assert mod_w.shape[0] == 2 and bsz <= 7 and tc % 128 == 0
    assert t % TOKEN_TILE == 0 and TOKEN_TILE % (8 * GRID_W) == 0 and (TOKEN_TILE * GRID_W) % (8 * t) == 0
    ctx_row = bsz

    cc = jnp.zeros((8, D), F32).at[:bsz].set(c).at[ctx_row].set(c_ctx)
    mods = _mods(cc, mod_w, mod_b)
    m0 = mods[0].reshape(8, 1, N_MOD * D)
    m1 = mods[1].reshape(8, 1, N_MOD * D)

    w0 = jnp.swapaxes(ab_w_in, 1, 2)
    gw = jnp.zeros((128, 512), F32)
    gw = gw.at[0:GLA_LR, 0:256].set(ab_gate_w[0, 0]).at[GLA_LR:2 * GLA_LR, 256:512].set(ab_gate_w[0, 1])
    gw = gw.astype(BF16)
    gb = ab_gate_b[0].reshape(1, 512)
    gng = ab_gla_norm_g[0].reshape(1, 512)
    vng = ab_vnorm_g[0].reshape(1, 512)
    sw = ab_spatial_w[0].astype(BF16)
    sbt = ab_spatial_b[0].T
    wo0 = ab_w_out[0].astype(BF16)
    n1g = [norm_g[i, 0].reshape(1, D) for i in range(2)]
    n2g = [norm_g[i, 1].reshape(1, D) for i in range(2)]
    wi = ffn_w_in.astype(BF16)
    wo2 = ffn_w_out.astype(BF16)

    tmc = _tile(tc, 256)
    tmx = TOKEN_TILE
    nctx = bsz * tc
    tmf = _tile(nctx, TOKEN_TILE)
    ctxf = ctx.reshape(1, nctx, D)
    pcf, lacf = _proj0(ctxf, m0, ctx_row, n1g[0], w0, gw, gb, tmf)
    px, lax_ = _proj0(x, m0, None, n1g[0], w0, gw, gb, tmx)
    ac, ax = _gla(pcf.reshape(bsz, tc, P0_COLS), lacf.reshape(bsz, tc, 512), px, lax_, gng)
    ctx1 = _post0(ctxf, m0, ctx_row, ac.reshape(1, nctx, 512), pcf, vng, sw, sbt, wo0, n2g[0], wi, wo2, 0,
                  tmf, False).reshape(bsz, tc, D)
    x1 = _post0(x, m0, None, ax, px, vng, sw, sbt, wo0, n2g[0], wi, wo2, 0, tmx, True)
    x1 = x1.reshape(bsz, t, D)

    w1 = jnp.swapaxes(ssd_w_in, 1, 2)
    cw = ssd_conv_w[0]
    cb = ssd_conv_b[0].reshape(1, SSD_XBC)
    dtb = jnp.concatenate([ssd_dt_bias[0].reshape(1, 2 * SSD_H), jnp.zeros((1, 64), F32)], axis=1)
    alog = jnp.concatenate([ssd_a_log[0].reshape(1, 2 * SSD_H), jnp.zeros((1, 64), F32)], axis=1)
    dexp = jnp.repeat(ssd_d[0], SSD_P).reshape(1, SSD_INNER)
    sng = ssd_norm_g[0].reshape(1, SSD_INNER)
    wo1 = ssd_w_out[0].astype(BF16)

    xb_c, dt_c = _proj1(ctx1, m1, ctx_row, n1g[1], w1, cw, cb, dtb, tmc, True)
    xbc_x, z_x, dt_x = _proj1(x1, m1, None, n1g[1], w1, cw, cb, dtb, tmx, False)
    yn = _ssd(xbc_x, dt_x, z_x, xb_c, dt_c, alog, dexp, sng)
    out = _post1(x1, m1, yn, wo1, n2g[1], wi, wo2, 1, final_norm_g.reshape(1, D), tmx)
    return out.reshape(bsz, t, D)
```
